```python
import math
import jax
import jax.numpy as jnp
from jax import lax
import numpy as np

D_MODEL = 1024
BATCH = 4
SEQ = 8192
DEPTH = 4
DEC_BATCH = 16
DEC_SEQ = 4096
PAST_LEN = 128

POOL_GROUPS = 4
POOL_GROUP_WIDTH = D_MODEL // 8
POOL_WIDTH = POOL_GROUPS * POOL_GROUP_WIDTH
POOL_WINDOWS = (2, 4, 8, 16)
DA_HEADS = 8
DA_HEAD_DIM = 64
DA_V_DIM = 2 * DA_HEAD_DIM
DA_WIDTH = DA_HEADS * DA_V_DIM
DA_QK_WIDTH = DA_HEADS * 2 * DA_HEAD_DIM
ROT_DIM = DA_HEAD_DIM // 4
ROPE_THETA = 500000.0
QBLOCK = 128
SUBLN_EPS = 1e-5
N_MEM = 256
XA_HEADS = 4
XA_HEAD_DIM = 128
XA_WIDTH = XA_HEADS * XA_HEAD_DIM
N_BRANCHES = 3
LN_EPS = 1e-5
IN_SPLITS = (POOL_WIDTH, POOL_WIDTH, DA_QK_WIDTH, DA_QK_WIDTH, DA_WIDTH, DA_WIDTH,
             XA_WIDTH, XA_WIDTH, N_BRANCHES * D_MODEL)
IN_COLS = sum(IN_SPLITS)
SPLIT_POINTS = [sum(IN_SPLITS[:i + 1]) for i in range(len(IN_SPLITS) - 1)]
DEEPNORM_ALPHA = (2.0 * DEPTH) ** 0.25
DEEPNORM_BETA = (8.0 * DEPTH) ** -0.25

kernel_name = "hybrid_pool_diffattn_memxattn_encoder"


def _layernorm(x, g, b):
    xf = x.astype(jnp.float32)
    mu = jnp.mean(xf, axis=-1, keepdims=True)
    var = jnp.mean(jnp.square(xf - mu), axis=-1, keepdims=True)
    y = (xf - mu) * lax.rsqrt(var + LN_EPS) * g.astype(jnp.float32) + b.astype(jnp.float32)
    return y.astype(x.dtype)


def _rope_tables(seq):
    pos = jnp.arange(seq, dtype=jnp.float32)
    inv = ROPE_THETA ** (-jnp.arange(0, ROT_DIM, 2, dtype=jnp.float32) / ROT_DIM)
    ang = pos[:, None] * inv[None, :]
    return jnp.cos(ang), jnp.sin(ang)


def _apply_partial_rope(x, cos, sin):
    half = ROT_DIM // 2
    c = cos[None, :, None, None, :].astype(x.dtype)
    s = sin[None, :, None, None, :].astype(x.dtype)
    x1 = x[..., :half]
    x2 = x[..., half:ROT_DIM]
    rot = jnp.concatenate([x1 * c - x2 * s, x2 * c + x1 * s], axis=-1)
    return jnp.concatenate([rot, x[..., ROT_DIM:]], axis=-1)


def _pool_mixer(u, pool_w, pool_scale):
    B, S, _ = u.shape
    uf = u.astype(jnp.float32)
    cs = jnp.concatenate([jnp.zeros((B, 1, POOL_WIDTH), jnp.float32), jnp.cumsum(uf, axis=1)], axis=1)
    t = jnp.arange(S)
    outs = []
    for g, w in enumerate(POOL_WINDOWS):
        lo = jnp.maximum(t - w // 2, 0)
        hi = jnp.minimum(t + w // 2 - 1, S - 1)
        csg = cs[..., g * POOL_GROUP_WIDTH:(g + 1) * POOL_GROUP_WIDTH]
        wsum = jnp.take(csg, hi + 1, axis=1) - jnp.take(csg, lo, axis=1)
        cnt = (hi - lo + 1).astype(jnp.float32)[None, :, None]
        outs.append(wsum / cnt)
    pooled = jnp.stack(outs, axis=2)
    d = (pooled - uf.reshape(B, S, POOL_GROUPS, POOL_GROUP_WIDTH)).astype(u.dtype)
    y = jnp.einsum('bsgc,gcd->bsgd', d, pool_w)
    return y.reshape(B, S, POOL_WIDTH) * pool_scale


def _diff_attention(q, k, v, lam, lam_init, subln_g):
    B, S, H, _, DH = q.shape
    nb = S // QBLOCK
    qb = q.reshape(B, nb, QBLOCK, H, 2, DH).transpose(1, 0, 2, 3, 4, 5)
    scale = DH ** -0.5

    def one_block(qblk):
        s = jnp.einsum('bqhcd,bkhcd->bhcqk', qblk, k).astype(jnp.float32) * scale
        p = jax.nn.softmax(s, axis=-1)
        a = p[:, :, 0] - lam * p[:, :, 1]
        return jnp.einsum('bhqk,bkhe->bqhe', a.astype(v.dtype), v)

    o = lax.map(one_block, qb)
    o = o.transpose(1, 0, 2, 3, 4).reshape(B, S, H, DA_V_DIM)
    of = o.astype(jnp.float32)
    of = of * lax.rsqrt(jnp.mean(jnp.square(of), axis=-1, keepdims=True) + SUBLN_EPS)
    of = of * subln_g.astype(jnp.float32) * (1.0 - lam_init)
    return of.astype(v.dtype)


def _layer(x, mem, cos, sin, lam_init, w_in, w_mem_kv, pool_w, pool_scale,
           lam_q1, lam_k1, lam_q2, lam_k2, subln_g, w_br_a, w_br_b, w_br_c, w_out, ln_g, ln_b):
    B, S, _ = x.shape
    h = jnp.einsum('bsd,de->bse', x, w_in)
    pu, pz, q, k, v, az, xq, xz, gl = jnp.split(h, SPLIT_POINTS, axis=-1)

    ya = _pool_mixer(pu, pool_w, pool_scale) * jax.nn.silu(pz)
    ya = jnp.einsum('bsc,cd->bsd', ya, w_br_a)

    q = _apply_partial_rope(q.reshape(B, S, DA_HEADS, 2, DA_HEAD_DIM), cos, sin)
    k = _apply_partial_rope(k.reshape(B, S, DA_HEADS, 2, DA_HEAD_DIM), cos, sin)
    v = v.reshape(B, S, DA_HEADS, DA_V_DIM)
    lam = (jnp.exp(jnp.sum(lam_q1.astype(jnp.float32) * lam_k1.astype(jnp.float32)))
           - jnp.exp(jnp.sum(lam_q2.astype(jnp.float32) * lam_k2.astype(jnp.float32)))
           + lam_init)
    yb = _diff_attention(q, k, v, lam, lam_init, subln_g).reshape(B, S, DA_WIDTH) * jax.nn.silu(az)
    yb = jnp.einsum('bsc,cd->bsd', yb, w_br_b)

    kv = jnp.einsum('bmd,de->bme', mem, w_mem_kv)
    km, vm = jnp.split(kv, 2, axis=-1)
    km = km.reshape(B, N_MEM, XA_HEADS, XA_HEAD_DIM)
    vm = vm.reshape(B, N_MEM, XA_HEADS, XA_HEAD_DIM)
    xq = xq.reshape(B, S, XA_HEADS, XA_HEAD_DIM)
    sc = jnp.einsum('bshd,bmhd->bhsm', xq, km).astype(jnp.float32) * (XA_HEAD_DIM ** -0.5)
    pm = jax.nn.softmax(sc, axis=-1).astype(vm.dtype)
    yc = jnp.einsum('bhsm,bmhd->bshd', pm, vm).reshape(B, S, XA_WIDTH) * jax.nn.silu(xz)
    yc = jnp.einsum('bsc,cd->bsd', yc, w_br_c)

    g = jax.nn.sigmoid(gl).reshape(B, S, N_BRANCHES, D_MODEL)
    merged = g[:, :, 0] * ya + g[:, :, 1] * yb + g[:, :, 2] * yc
    out = jnp.einsum('bsd,de->bse', merged, w_out)
    return _layernorm(DEEPNORM_ALPHA * x + out, ln_g, ln_b)


def _trunk(x, mem, ln_in_g, ln_in_b, w_in, w_mem_kv, pool_w, pool_scale,
           lam_q1, lam_k1, lam_q2, lam_k2, subln_g, w_br_a, w_br_b, w_br_c, w_out, ln_g, ln_b):
    cos, sin = _rope_tables(x.shape[1])
    x = _layernorm(x, ln_in_g, ln_in_b)
    for i in range(DEPTH):
        lam_init = 0.8 - 0.6 * math.exp(-0.3 * i)
        x = _layer(x, mem, cos, sin, lam_init, w_in[i], w_mem_kv[i], pool_w[i], pool_scale[i],
                   lam_q1[i], lam_k1[i], lam_q2[i], lam_k2[i], subln_g[i],
                   w_br_a[i], w_br_b[i], w_br_c[i], w_out[i], ln_g[i], ln_b[i])
    return x


def setup_inputs(seed: int = 0) -> dict:
    key = jax.random.key(seed)
    ks = jax.random.split(key, 24)
    f32 = jnp.float32
    x_prompt = jax.random.normal(ks[0], (BATCH, SEQ, D_MODEL), f32)
    x_sample = jax.random.normal(ks[1], (DEC_BATCH, DEC_SEQ, D_MODEL), f32)
    mem_prompt = jax.random.normal(ks[2], (BATCH, N_MEM, D_MODEL), f32)
    mem_sample = jax.random.normal(ks[3], (DEC_BATCH, N_MEM, D_MODEL), f32)
    ln_in_g = 1.0 + 0.02 * jax.random.normal(ks[4], (D_MODEL,), f32)
    ln_in_b = 0.02 * jax.random.normal(ks[5], (D_MODEL,), f32)
    col_scale = jnp.concatenate([
        jnp.ones((sum(IN_SPLITS[:4]),), f32),
        jnp.full((DA_WIDTH,), DEEPNORM_BETA, f32),
        jnp.ones((sum(IN_SPLITS[5:]),), f32)])
    w_in = jax.random.normal(ks[6], (DEPTH, D_MODEL, IN_COLS), f32) * (D_MODEL ** -0.5) * col_scale
    mem_scale = jnp.concatenate([jnp.ones((XA_WIDTH,), f32), jnp.full((XA_WIDTH,), DEEPNORM_BETA, f32)])
    w_mem_kv = jax.random.normal(ks[7], (DEPTH, D_MODEL, 2 * XA_WIDTH), f32) * (D_MODEL ** -0.5) * mem_scale
    pool_w = jax.random.normal(ks[8], (DEPTH, POOL_GROUPS, POOL_GROUP_WIDTH, POOL_GROUP_WIDTH), f32) * (POOL_GROUP_WIDTH ** -0.5)
    pool_scale = 1.0 + 0.1 * jax.random.normal(ks[9], (DEPTH, POOL_WIDTH), f32)
    lam_q1 = 0.1 * jax.random.normal(ks[10], (DEPTH, DA_HEAD_DIM), f32)
    lam_k1 = 0.1 * jax.random.normal(ks[11], (DEPTH, DA_HEAD_DIM), f32)
    lam_q2 = 0.1 * jax.random.normal(ks[12], (DEPTH, DA_HEAD_DIM), f32)
    lam_k2 = 0.1 * jax.random.normal(ks[13], (DEPTH, DA_HEAD_DIM), f32)
    subln_g = 1.0 + 0.02 * jax.random.normal(ks[14], (DEPTH, DA_V_DIM), f32)
    w_br_a = jax.random.normal(ks[15], (DEPTH, POOL_WIDTH, D_MODEL), f32) * (POOL_WIDTH ** -0.5) * DEEPNORM_BETA
    w_br_b = jax.random.normal(ks[16], (DEPTH, DA_WIDTH, D_MODEL), f32) * (DA_WIDTH ** -0.5) * DEEPNORM_BETA
    w_br_c = jax.random.normal(ks[17], (DEPTH, XA_WIDTH, D_MODEL), f32) * (XA_WIDTH ** -0.5) * DEEPNORM_BETA
    w_out = jax.random.normal(ks[18], (DEPTH, D_MODEL, D_MODEL), f32) * (D_MODEL ** -0.5) * DEEPNORM_BETA
    ln_g = 1.0 + 0.02 * jax.random.normal(ks[19], (DEPTH, D_MODEL), f32)
    ln_b = 0.02 * jax.random.normal(ks[20], (DEPTH, D_MODEL), f32)
    return {"x_prompt": x_prompt, "x_sample": x_sample, "mem_prompt": mem_prompt, "mem_sample": mem_sample,
            "ln_in_g": ln_in_g, "ln_in_b": ln_in_b, "w_in": w_in, "w_mem_kv": w_mem_kv,
            "pool_w": pool_w, "pool_scale": pool_scale, "lam_q1": lam_q1, "lam_k1": lam_k1,
            "lam_q2": lam_q2, "lam_k2": lam_k2, "subln_g": subln_g, "w_br_a": w_br_a,
            "w_br_b": w_br_b, "w_br_c": w_br_c, "w_out": w_out, "ln_g": ln_g, "ln_b": ln_b}


def reference(x_prompt, x_sample, mem_prompt, mem_sample, ln_in_g, ln_in_b, w_in, w_mem_kv,
              pool_w, pool_scale, lam_q1, lam_k1, lam_q2, lam_k2, subln_g, w_br_a, w_br_b,
              w_br_c, w_out, ln_g, ln_b):
    y_prompt = _trunk(x_prompt, mem_prompt, ln_in_g, ln_in_b, w_in, w_mem_kv, pool_w, pool_scale,
                      lam_q1, lam_k1, lam_q2, lam_k2, subln_g, w_br_a, w_br_b, w_br_c, w_out, ln_g, ln_b)
    y_sample = _trunk(x_sample, mem_sample, ln_in_g, ln_in_b, w_in, w_mem_kv, pool_w, pool_scale,
                      lam_q1, lam_k1, lam_q2, lam_k2, subln_g, w_br_a, w_br_b, w_br_c, w_out, ln_g, ln_b)
    return (y_prompt, y_sample)
```

```python
import functools
import math

import jax
import jax.numpy as jnp
from jax import lax
from jax.experimental import pallas as pl
from jax.experimental.pallas import tpu as pltpu

F32 = jnp.float32
BF16 = jnp.bfloat16

D_MODEL = 1024
POOL_GROUPS = 4
POOL_GROUP_WIDTH = 128
POOL_WIDTH = POOL_GROUPS * POOL_GROUP_WIDTH
POOL_WINDOWS = (2, 4, 8, 16)
POOL_HALO = 8
DA_HEADS = 8
DA_HEAD_DIM = 64
DA_V_DIM = 2 * DA_HEAD_DIM
DA_WIDTH = DA_HEADS * DA_V_DIM
ROT_DIM = DA_HEAD_DIM // 4
ROPE_THETA = 500000.0
SUBLN_EPS = 1e-5
N_MEM = 256
XA_HEADS = 4
XA_HEAD_DIM = 128
XA_WIDTH = XA_HEADS * XA_HEAD_DIM
LN_EPS = 1e-5
LOG2E = 1.4426950408889634

C_PU, C_PZ, C_Q, C_K, C_V, C_AZ, C_XQ, C_XZ, C_GL, C_END = 0, 512, 1024, 2048, 3072, 4096, 5120, 5632, 6144, 9216

V_ROWS = DA_V_DIM + 16
LANES = 128
VMEM_LIMIT = 56 * 1024 * 1024


def _cparams(sem):
    return pltpu.CompilerParams(dimension_semantics=sem, vmem_limit_bytes=VMEM_LIMIT)


def _const_spec(shape):
    n = len(shape)
    return pl.BlockSpec(shape, lambda *_: (0,) * n, pipeline_mode=pl.Buffered(1))


def _ln_rows(x, g, b):
    mu = jnp.mean(x, axis=-1, keepdims=True)
    xc = x - mu
    var = jnp.mean(xc * xc, axis=-1, keepdims=True)
    return xc * lax.rsqrt(var + LN_EPS) * g + b


def _ln_kernel(x_ref, g_ref, b_ref, y_ref):
    y_ref[...] = _ln_rows(x_ref[...], g_ref[...], b_ref[...])


def _layernorm(x, g, b, tm):
    B, S, D = x.shape
    return pl.pallas_call(
        _ln_kernel,
        grid=(B, S // tm),
        in_specs=[pl.BlockSpec((None, tm, D), lambda b_, i: (b_, i, 0)),
                  _const_spec((1, D)), _const_spec((1, D))],
        out_specs=pl.BlockSpec((None, tm, D), lambda b_, i: (b_, i, 0)),
        out_shape=jax.ShapeDtypeStruct((B, S, D), F32),
        compiler_params=_cparams(("parallel", "parallel")),
        name="ln_in",
    )(x, g.reshape(1, D), b.reshape(1, D))


def _proj_kernel(x_ref, w_ref, c_ref, sa_ref, sb_ref, pu_ref, q_ref, k_ref, vt_ref):
    x = x_ref[...].astype(BF16)
    tm = x.shape[0]
    pu_ref[...] = jnp.dot(x, w_ref[:, 0:POOL_WIDTH], preferred_element_type=F32)
    qk = jnp.dot(x, w_ref[:, POOL_WIDTH:POOL_WIDTH + 2 * DA_WIDTH], preferred_element_type=F32)
    c = c_ref[...]
    sa = sa_ref[...]
    sb = sb_ref[...]
    q_scale = (DA_HEAD_DIM ** -0.5) * LOG2E
    for h in range(2 * DA_HEADS):
        blk = qk[:, h * LANES:(h + 1) * LANES]
        rot = blk * c + pltpu.roll(blk, 8, axis=1) * sa + pltpu.roll(blk, LANES - 8, axis=1) * sb
        if h < DA_HEADS:
            q_ref[:, h * LANES:(h + 1) * LANES] = (rot * q_scale).astype(BF16)
        else:
            hh = h - DA_HEADS
            k_ref[:, hh * LANES:(hh + 1) * LANES] = rot.astype(BF16)
    v = jnp.dot(x, w_ref[:, POOL_WIDTH + 2 * DA_WIDTH:POOL_WIDTH + 3 * DA_WIDTH], preferred_element_type=F32)
    ones = jnp.ones((V_ROWS - DA_V_DIM, tm), BF16)
    for h in range(DA_HEADS):
        vt_ref[h, 0:DA_V_DIM, :] = v[:, h * LANES:(h + 1) * LANES].T.astype(BF16)
        vt_ref[h, DA_V_DIM:V_ROWS, :] = ones


def _project(x, w_a, rope_c, rope_sa, rope_sb, tm):
    B, S, D = x.shape
    n_a = w_a.shape[1]
    return pl.pallas_call(
        _proj_kernel,
        grid=(B, S // tm),
        in_specs=[pl.BlockSpec((None, tm, D), lambda b_, i: (b_, i, 0)),
                  _const_spec((D, n_a)),
                  pl.BlockSpec((tm, LANES), lambda b_, i: (i, 0)),
                  pl.BlockSpec((tm, LANES), lambda b_, i: (i, 0)),
                  pl.BlockSpec((tm, LANES), lambda b_, i: (i, 0))],
        out_specs=[pl.BlockSpec((None, tm, POOL_WIDTH), lambda b_, i: (b_, i, 0)),
                   pl.BlockSpec((None, tm, DA_WIDTH), lambda b_, i: (b_, i, 0)),
                   pl.BlockSpec((None, tm, DA_WIDTH), lambda b_, i: (b_, i, 0)),
                   pl.BlockSpec((None, DA_HEADS, V_ROWS, tm), lambda b_, i: (b_, 0, 0, i))],
        out_shape=[jax.ShapeDtypeStruct((B, S, POOL_WIDTH), F32),
                   jax.ShapeDtypeStruct((B, S, DA_WIDTH), BF16),
                   jax.ShapeDtypeStruct((B, S, DA_WIDTH), BF16),
                   jax.ShapeDtypeStruct((B, DA_HEADS, V_ROWS, S), BF16)],
        compiler_params=_cparams(("parallel", "parallel")),
        name="proj_qkv",
    )(x, w_a, rope_c, rope_sa, rope_sb)


def _attn_kernel(lam_ref, q_ref, k_ref, vt_ref, g_ref, o_ref, *, tk, qb):
    tq = q_ref.shape[0]
    S = k_ref.shape[0]
    lam = lam_ref[0]
    lane = lax.broadcasted_iota(jnp.int32, (qb, LANES), 1)

    def q_body(qs, carry):
        r0 = pl.multiple_of(qs * qb, qb)
        q = q_ref[pl.ds(r0, qb), :].astype(F32)
        wq = jnp.concatenate([jnp.where(lane < DA_HEAD_DIM, q, 0.0),
                              jnp.where(lane >= DA_HEAD_DIM, q, 0.0)], axis=0).astype(BF16)

        def kv_body(j, c):
            m, acc = c
            c0 = pl.multiple_of(j * tk, tk)
            kt = k_ref[pl.ds(c0, tk), :]
            s = lax.dot_general(kt, wq, (((1,), (1,)), ((), ())), preferred_element_type=F32)
            m_new = jnp.maximum(m, jnp.max(s, axis=0, keepdims=True))
            alpha = jnp.exp2(m - m_new)
            p = jnp.exp2(s - m_new).astype(BF16)
            pv = jnp.dot(vt_ref[:, pl.ds(c0, tk)], p, preferred_element_type=F32)
            return m_new, acc * alpha + pv

        m0 = jnp.full((1, 2 * qb), -1e30, F32)
        acc0 = jnp.zeros((V_ROWS, 2 * qb), F32)
        _, acc = lax.fori_loop(0, S // tk, kv_body, (m0, acc0))
        o0 = acc[0:DA_V_DIM, 0:qb] / acc[DA_V_DIM:DA_V_DIM + 1, 0:qb]
        o1 = acc[0:DA_V_DIM, qb:2 * qb] / acc[DA_V_DIM:DA_V_DIM + 1, qb:2 * qb]
        o = o0 - lam * o1
        o = o * lax.rsqrt(jnp.mean(o * o, axis=0, keepdims=True) + SUBLN_EPS)
        o_ref[pl.ds(r0, qb), :] = (o * g_ref[...]).T
        return carry

    lax.fori_loop(0, tq // qb, q_body, 0)


def _diff_attention(q, k, vt, lam_arr, g_mat, tq, tk):
    B, S, _ = q.shape
    kern = functools.partial(_attn_kernel, tk=tk, qb=LANES)
    return pl.pallas_call(
        kern,
        grid=(B, DA_HEADS, S // tq),
        in_specs=[pl.BlockSpec(memory_space=pltpu.SMEM),
                  pl.BlockSpec((None, tq, LANES), lambda b_, h, i: (b_, i, h)),
                  pl.BlockSpec((None, S, LANES), lambda b_, h, i: (b_, 0, h)),
                  pl.BlockSpec((None, None, V_ROWS, S), lambda b_, h, i: (b_, h, 0, 0)),
                  pl.BlockSpec((DA_V_DIM, LANES), lambda b_, h, i: (0, 0))],
        out_specs=pl.BlockSpec((None, tq, LANES), lambda b_, h, i: (b_, i, h)),
        out_shape=jax.ShapeDtypeStruct((B, S, DA_WIDTH), F32),
        compiler_params=_cparams(("parallel", "parallel", "parallel")),
        name="diff_attn",
    )(lam_arr, q, k, vt, g_mat)


def _memkv_kernel(m_ref, w_ref, km_ref, vm_ref):
    kv = jnp.dot(m_ref[...].astype(BF16), w_ref[...], preferred_element_type=F32)
    km_ref[...] = kv[:, 0:XA_WIDTH].astype(BF16)
    vm_ref[...] = kv[:, XA_WIDTH:2 * XA_WIDTH].astype(BF16)


def _mem_kv(mem, w_mem_kv):
    B, M, D = mem.shape
    return pl.pallas_call(
        _memkv_kernel,
        grid=(B,),
        in_specs=[pl.BlockSpec((None, M, D), lambda b_: (b_, 0, 0)),
                  _const_spec((D, 2 * XA_WIDTH))],
        out_specs=[pl.BlockSpec((None, M, XA_WIDTH), lambda b_: (b_, 0, 0)),
                   pl.BlockSpec((None, M, XA_WIDTH), lambda b_: (b_, 0, 0))],
        out_shape=[jax.ShapeDtypeStruct((B, M, XA_WIDTH), BF16),
                   jax.ShapeDtypeStruct((B, M, XA_WIDTH), BF16)],
        compiler_params=_cparams(("parallel",)),
        name="mem_kv",
    )(mem, w_mem_kv)


def _silu(z):
    return z * (1.0 / (1.0 + jnp.exp(-z)))


def _sigmoid(z):
    return 1.0 / (1.0 + jnp.exp(-z))


def _merge_kernel(x_ref, pu_ref, pup_ref, pun_ref, o_ref, km_ref, vm_ref, wg_ref, pw_ref, ps_ref,
                  wa_ref, wb_ref, wc_ref, wo_ref, g_ref, b_ref, y_ref, ue_ref, *, seq_len, alpha):
    i = pl.program_id(1)
    n_i = pl.num_programs(1)
    tm = x_ref.shape[0]
    x = x_ref[...]
    xb = x.astype(BF16)

    u = pu_ref[...]
    ue_ref[0:POOL_HALO, :] = jnp.where(i > 0, pup_ref[...], 0.0)
    ue_ref[POOL_HALO:POOL_HALO + tm, :] = u
    ue_ref[POOL_HALO + tm:2 * POOL_HALO + tm, :] = jnp.where(i < n_i - 1, pun_ref[...], 0.0)
    t = lax.broadcasted_iota(jnp.int32, (tm, POOL_GROUP_WIDTH), 0) + i * tm
    ya_parts = []
    for g, w in enumerate(POOL_WINDOWS):
        cols = slice(g * POOL_GROUP_WIDTH, (g + 1) * POOL_GROUP_WIDTH)
        wsum = ue_ref[POOL_HALO - w // 2:POOL_HALO - w // 2 + tm, cols]
        for j in range(1, w):
            r = POOL_HALO - w // 2 + j
            wsum = wsum + ue_ref[r:r + tm, cols]
        cnt = jnp.minimum(t + (w // 2 - 1), seq_len - 1) - jnp.maximum(t - w // 2, 0) + 1
        d = (wsum / cnt.astype(F32) - u[:, cols]).astype(BF16)
        ya_parts.append(jnp.dot(d, pw_ref[g], preferred_element_type=F32))
    ya = jnp.concatenate(ya_parts, axis=1) * ps_ref[...]
    ya = ya * _silu(jnp.dot(xb, wg_ref[:, 0:512], preferred_element_type=F32))
    ya = jnp.dot(ya.astype(BF16), wa_ref[...], preferred_element_type=F32)

    yb = o_ref[...] * _silu(jnp.dot(xb, wg_ref[:, 512:1536], preferred_element_type=F32))
    yb = jnp.dot(yb.astype(BF16), wb_ref[...], preferred_element_type=F32)

    xq = jnp.dot(xb, wg_ref[:, 1536:2048], preferred_element_type=F32).astype(BF16)
    yc_parts = []
    for h in range(XA_HEADS):
        cols = slice(h * XA_HEAD_DIM, (h + 1) * XA_HEAD_DIM)
        sc = lax.dot_general(xq[:, cols], km_ref[:, cols], (((1,), (1,)), ((), ())),
                             preferred_element_type=F32) * (XA_HEAD_DIM ** -0.5)
        e = jnp.exp(sc - jnp.max(sc, axis=-1, keepdims=True))
        pm = (e / jnp.sum(e, axis=-1, keepdims=True)).astype(BF16)
        yc_parts.append(jnp.dot(pm, vm_ref[:, cols], preferred_element_type=F32))
    yc = jnp.concatenate(yc_parts, axis=1)
    yc = yc * _silu(jnp.dot(xb, wg_ref[:, 2048:2560], preferred_element_type=F32))
    yc = jnp.dot(yc.astype(BF16), wc_ref[...], preferred_element_type=F32)

    merged = _sigmoid(jnp.dot(xb, wg_ref[:, 2560:3584], preferred_element_type=F32)) * ya
    merged = merged + _sigmoid(jnp.dot(xb, wg_ref[:, 3584:4608], preferred_element_type=F32)) * yb
    merged = merged + _sigmoid(jnp.dot(xb, wg_ref[:, 4608:5632], preferred_element_type=F32)) * yc
    out = jnp.dot(merged.astype(BF16), wo_ref[...], preferred_element_type=F32)
    y_ref[...] = _ln_rows(alpha * x + out, g_ref[...], b_ref[...])


def _merge(x, pu, o, km, vm, w_g, pool_w, pool_scale, w_a, w_b, w_c, w_o, ln_g, ln_b, tm, alpha):
    B, S, D = x.shape
    hb = tm // POOL_HALO
    n_hb = S // POOL_HALO
    kern = functools.partial(_merge_kernel, seq_len=S, alpha=alpha)
    tile = lambda w: pl.BlockSpec((None, tm, w), lambda b_, i: (b_, i, 0))
    return pl.pallas_call(
        kern,
        grid=(B, S // tm),
        in_specs=[tile(D),
                  tile(POOL_WIDTH),
                  pl.BlockSpec((None, POOL_HALO, POOL_WIDTH), lambda b_, i: (b_, jnp.maximum(i * hb - 1, 0), 0)),
                  pl.BlockSpec((None, POOL_HALO, POOL_WIDTH),
                               lambda b_, i: (b_, jnp.minimum((i + 1) * hb, n_hb - 1), 0)),
                  tile(DA_WIDTH),
                  pl.BlockSpec((None, N_MEM, XA_WIDTH), lambda b_, i: (b_, 0, 0)),
                  pl.BlockSpec((None, N_MEM, XA_WIDTH), lambda b_, i: (b_, 0, 0)),
                  _const_spec(w_g.shape),
                  _const_spec(pool_w.shape),
                  _const_spec((1, POOL_WIDTH)),
                  _const_spec(w_a.shape), _const_spec(w_b.shape), _const_spec(w_c.shape), _const_spec(w_o.shape),
                  _const_spec((1, D)), _const_spec((1, D))],
        out_specs=tile(D),
        out_shape=jax.ShapeDtypeStruct((B, S, D), F32),
        scratch_shapes=[pltpu.VMEM((tm + 2 * POOL_HALO, POOL_WIDTH), F32)],
        compiler_params=_cparams(("parallel", "arbitrary")),
        name="merge",
    )(x, pu, pu, pu, o, km, vm, w_g, pool_w, pool_scale.reshape(1, POOL_WIDTH), w_a, w_b, w_c, w_o,
      ln_g.reshape(1, D), ln_b.reshape(1, D))


def _rope_tables(seq):
    pos = jnp.arange(seq, dtype=F32)
    inv = ROPE_THETA ** (-jnp.arange(0, ROT_DIM, 2, dtype=F32) / ROT_DIM)
    ang = pos[:, None] * inv[None, :]
    cos, sin = jnp.cos(ang), jnp.sin(ang)
    half = ROT_DIM // 2
    lane = jnp.arange(LANES) % DA_HEAD_DIM
    idx = lane % half
    c = jnp.where(lane[None, :] < ROT_DIM, cos[:, idx], 1.0)
    sa = jnp.where((lane[None, :] >= half) & (lane[None, :] < ROT_DIM), sin[:, idx], 0.0)
    sb = jnp.where(lane[None, :] < half, -sin[:, idx], 0.0)
    return c.astype(F32), sa.astype(F32), sb.astype(F32)


def _tiles(S):
    return dict(tm_ln=512, tm_proj=512, tq=512, tk=512, tm_merge=256)


def _trunk(x, mem, ln_in_g, ln_in_b, layers):
    B, S, _ = x.shape
    depth = len(layers)
    alpha = (2.0 * depth) ** 0.25
    t = _tiles(S)
    rope_c, rope_sa, rope_sb = _rope_tables(S)
    x = _layernorm(x, ln_in_g, ln_in_b, t["tm_ln"])
    for L in layers:
        pu, q, k, vt = _project(x, L["w_a"], rope_c, rope_sa, rope_sb, t["tm_proj"])
        o = _diff_attention(q, k, vt, L["lam"], L["subln"], t["tq"], t["tk"])
        km, vm = _mem_kv(mem, L["w_mem_kv"])
        x = _merge(x, pu, o, km, vm, L["w_g"], L["pool_w"], L["pool_scale"], L["w_br_a"], L["w_br_b"],
                   L["w_br_c"], L["w_out"], L["ln_g"], L["ln_b"], t["tm_merge"], alpha)
    return x


def _prepare_layers(w_in, w_mem_kv, pool_w, pool_scale, lam_q1, lam_k1, lam_q2, lam_k2, subln_g,
                    w_br_a, w_br_b, w_br_c, w_out, ln_g, ln_b):
    layers = []
    for i in range(w_in.shape[0]):
        lam_init = 0.8 - 0.6 * math.exp(-0.3 * i)
        lam = (jnp.exp(jnp.sum(lam_q1[i] * lam_k1[i])) - jnp.exp(jnp.sum(lam_q2[i] * lam_k2[i])) + lam_init)
        wi = w_in[i].astype(BF16)
        layers.append(dict(
            w_a=jnp.concatenate([wi[:, C_PU:C_PZ], wi[:, C_Q:C_AZ]], axis=1),
            w_g=jnp.concatenate([wi[:, C_PZ:C_Q], wi[:, C_AZ:C_END]], axis=1),
            w_mem_kv=w_mem_kv[i].astype(BF16),
            pool_w=pool_w[i].astype(BF16),
            pool_scale=pool_scale[i],
            lam=jnp.reshape(lam, (1,)).astype(F32),
            subln=jnp.broadcast_to((subln_g[i] * (1.0 - lam_init))[:, None], (DA_V_DIM, LANES)).astype(F32),
            w_br_a=w_br_a[i].astype(BF16), w_br_b=w_br_b[i].astype(BF16), w_br_c=w_br_c[i].astype(BF16),
            w_out=w_out[i].astype(BF16), ln_g=ln_g[i], ln_b=ln_b[i]))
    return layers


def kernel(x_prompt, x_sample, mem_prompt, mem_sample, ln_in_g, ln_in_b, w_in, w_mem_kv, pool_w, pool_scale,
           lam_q1, lam_k1, lam_q2, lam_k2, subln_g, w_br_a, w_br_b, w_br_c, w_out, ln_g, ln_b):
    layers = _prepare_layers(w_in, w_mem_kv, pool_w, pool_scale, lam_q1, lam_k1, lam_q2, lam_k2, subln_g,
                             w_br_a, w_br_b, w_br_c, w_out, ln_g, ln_b)
    y_prompt = _trunk(x_prompt, mem_prompt, ln_in_g, ln_in_b, layers)
    y_sample = _trunk(x_sample, mem_sample, ln_in_g, ln_in_b, layers)
    return (y_prompt, y_sample)
```

```python
import functools
import math

import jax
import jax.numpy as jnp
from jax import lax
from jax.experimental import pallas as pl
from jax.experimental.pallas import tpu as pltpu

F32 = jnp.float32
BF16 = jnp.bfloat16

D_MODEL = 1024
POOL_GROUPS = 4
POOL_GROUP_WIDTH = 128
POOL_WIDTH = POOL_GROUPS * POOL_GROUP_WIDTH
POOL_WINDOWS = (2, 4, 8, 16)
POOL_HALO = 8
DA_HEADS = 8
DA_HEAD_DIM = 64
DA_V_DIM = 2 * DA_HEAD_DIM
DA_WIDTH = DA_HEADS * DA_V_DIM
ROT_DIM = DA_HEAD_DIM // 4
ROPE_THETA = 500000.0
SUBLN_EPS = 1e-5
N_MEM = 256
XA_HEADS = 4
XA_HEAD_DIM = 128
XA_WIDTH = XA_HEADS * XA_HEAD_DIM
LN_EPS = 1e-5
LOG2E = 1.4426950408889634

C_PU, C_PZ, C_Q, C_K, C_V, C_AZ, C_XQ, C_XZ, C_GL, C_END = 0, 512, 1024, 2048, 3072, 4096, 5120, 5632, 6144, 9216

V_ROWS = DA_V_DIM + 16
LANES = 128
VMEM_LIMIT = 56 * 1024 * 1024


def _cparams(sem):
    return pltpu.CompilerParams(dimension_semantics=sem, vmem_limit_bytes=VMEM_LIMIT)


def _const_spec(shape):
    n = len(shape)
    return pl.BlockSpec(shape, lambda *_: (0,) * n, pipeline_mode=pl.Buffered(1))


def _ln_rows(x, g, b):
    mu = jnp.mean(x, axis=-1, keepdims=True)
    xc = x - mu
    var = jnp.mean(xc * xc, axis=-1, keepdims=True)
    return xc * lax.rsqrt(var + LN_EPS) * g + b


def _ln_kernel(x_ref, g_ref, b_ref, y_ref):
    y_ref[...] = _ln_rows(x_ref[...], g_ref[...], b_ref[...])


def _layernorm(x, g, b, tm):
    B, S, D = x.shape
    return pl.pallas_call(
        _ln_kernel,
        grid=(B, S // tm),
        in_specs=[pl.BlockSpec((None, tm, D), lambda b_, i: (b_, i, 0)),
                  _const_spec((1, D)), _const_spec((1, D))],
        out_specs=pl.BlockSpec((None, tm, D), lambda b_, i: (b_, i, 0)),
        out_shape=jax.ShapeDtypeStruct((B, S, D), F32),
        compiler_params=_cparams(("parallel", "parallel")),
        name="ln_in",
    )(x, g.reshape(1, D), b.reshape(1, D))


def _proj_kernel(x_ref, w_ref, c_ref, sa_ref, sb_ref, pu_ref, q_ref, k_ref, vt_ref):
    x = x_ref[...].astype(BF16)
    tm = x.shape[0]
    pu_ref[...] = jnp.dot(x, w_ref[:, 0:POOL_WIDTH], preferred_element_type=F32)
    qk = jnp.dot(x, w_ref[:, POOL_WIDTH:POOL_WIDTH + 2 * DA_WIDTH], preferred_element_type=F32)
    c = c_ref[...]
    sa = sa_ref[...]
    sb = sb_ref[...]
    q_scale = (DA_HEAD_DIM ** -0.5) * LOG2E
    for h in range(2 * DA_HEADS):
        blk = qk[:, h * LANES:(h + 1) * LANES]
        rot = blk * c + pltpu.roll(blk, 8, axis=1) * sa + pltpu.roll(blk, LANES - 8, axis=1) * sb
        if h < DA_HEADS:
            q_ref[:, h * LANES:(h + 1) * LANES] = (rot * q_scale).astype(BF16)
        else:
            hh = h - DA_HEADS
            k_ref[:, hh * LANES:(hh + 1) * LANES] = rot.astype(BF16)
    v = jnp.dot(x, w_ref[:, POOL_WIDTH + 2 * DA_WIDTH:POOL_WIDTH + 3 * DA_WIDTH], preferred_element_type=F32)
    ones = jnp.ones((V_ROWS - DA_V_DIM, tm), BF16)
    for h in range(DA_HEADS):
        vt_ref[h, 0:DA_V_DIM, :] = v[:, h * LANES:(h + 1) * LANES].T.astype(BF16)
        vt_ref[h, DA_V_DIM:V_ROWS, :] = ones


def _project(x, w_a, rope_c, rope_sa, rope_sb, tm):
    B, S, D = x.shape
    n_a = w_a.shape[1]
    return pl.pallas_call(
        _proj_kernel,
        grid=(B, S // tm),
        in_specs=[pl.BlockSpec((None, tm, D), lambda b_, i: (b_, i, 0)),
                  _const_spec((D, n_a)),
                  pl.BlockSpec((tm, LANES), lambda b_, i: (i, 0)),
                  pl.BlockSpec((tm, LANES), lambda b_, i: (i, 0)),
                  pl.BlockSpec((tm, LANES), lambda b_, i: (i, 0))],
        out_specs=[pl.BlockSpec((None, tm, POOL_WIDTH), lambda b_, i: (b_, i, 0)),
                   pl.BlockSpec((None, tm, DA_WIDTH), lambda b_, i: (b_, i, 0)),
                   pl.BlockSpec((None, tm, DA_WIDTH), lambda b_, i: (b_, i, 0)),
                   pl.BlockSpec((None, DA_HEADS, V_ROWS, tm), lambda b_, i: (b_, 0, 0, i))],
        out_shape=[jax.ShapeDtypeStruct((B, S, POOL_WIDTH), F32),
                   jax.ShapeDtypeStruct((B, S, DA_WIDTH), BF16),
                   jax.ShapeDtypeStruct((B, S, DA_WIDTH), BF16),
                   jax.ShapeDtypeStruct((B, DA_HEADS, V_ROWS, S), BF16)],
        compiler_params=_cparams(("parallel", "parallel")),
        name="proj_qkv",
    )(x, w_a, rope_c, rope_sa, rope_sb)


def _attn_kernel(lam_ref, q_ref, k_ref, vt_ref, g_ref, o_ref, wq_ref, m_ref, acc_ref, s_ref, *, tk, qb):
    tq = q_ref.shape[0]
    S = k_ref.shape[0]
    n_g = tq // qb
    lam = lam_ref[0]
    lane = lax.broadcasted_iota(jnp.int32, (LANES, qb), 0)

    for g in range(n_g):
        qt = q_ref[g * qb:(g + 1) * qb, :].astype(F32).T
        wq_ref[g] = jnp.concatenate([jnp.where(lane < DA_HEAD_DIM, qt, 0.0),
                                     jnp.where(lane >= DA_HEAD_DIM, qt, 0.0)], axis=1).astype(BF16)
        m_ref[g] = jnp.full((1, 2 * qb), -1e30, F32)
        acc_ref[g] = jnp.zeros((V_ROWS, 2 * qb), F32)

    n_t = S // tk

    def scores(tile, buf):
        c0 = pl.multiple_of(tile * tk, tk)
        kt = k_ref[pl.ds(c0, tk), :]
        for g in range(n_g):
            s_ref[buf, g] = jnp.dot(kt, wq_ref[g], preferred_element_type=F32)

    def softmax_pv(tile, buf):
        c0 = pl.multiple_of(tile * tk, tk)
        vt = vt_ref[:, pl.ds(c0, tk)]
        for g in range(n_g):
            s = s_ref[buf, g]
            m_old = m_ref[g]
            m_new = jnp.maximum(m_old, jnp.max(s, axis=0, keepdims=True))
            m_ref[g] = m_new
            p = jnp.exp2(s - m_new).astype(BF16)
            pv = jnp.dot(vt, p, preferred_element_type=F32)
            acc_ref[g] = acc_ref[g] * jnp.exp2(m_old - m_new) + pv

    scores(0, 0)

    def kv_pair(jj, carry):
        scores(2 * jj + 1, 1)
        softmax_pv(2 * jj, 0)
        scores(2 * jj + 2, 0)
        softmax_pv(2 * jj + 1, 1)
        return carry

    lax.fori_loop(0, n_t // 2 - 1, kv_pair, 0)
    scores(n_t - 1, 1)
    softmax_pv(n_t - 2, 0)
    softmax_pv(n_t - 1, 1)

    for g in range(n_g):
        acc = acc_ref[g]
        o0 = acc[0:DA_V_DIM, 0:qb] / acc[DA_V_DIM:DA_V_DIM + 1, 0:qb]
        o1 = acc[0:DA_V_DIM, qb:2 * qb] / acc[DA_V_DIM:DA_V_DIM + 1, qb:2 * qb]
        o = o0 - lam * o1
        o = o * lax.rsqrt(jnp.mean(o * o, axis=0, keepdims=True) + SUBLN_EPS)
        o_ref[g * qb:(g + 1) * qb, :] = (o * g_ref[...]).T


def _diff_attention(q, k, vt, lam_arr, g_mat, tq, tk):
    B, S, _ = q.shape
    n_g = tq // LANES
    kern = functools.partial(_attn_kernel, tk=tk, qb=LANES)
    return pl.pallas_call(
        kern,
        grid=(B, DA_HEADS, S // tq),
        in_specs=[pl.BlockSpec(memory_space=pltpu.SMEM),
                  pl.BlockSpec((None, tq, LANES), lambda b_, h, i: (b_, i, h)),
                  pl.BlockSpec((None, S, LANES), lambda b_, h, i: (b_, 0, h)),
                  pl.BlockSpec((None, None, V_ROWS, S), lambda b_, h, i: (b_, h, 0, 0)),
                  pl.BlockSpec((DA_V_DIM, LANES), lambda b_, h, i: (0, 0))],
        out_specs=pl.BlockSpec((None, tq, LANES), lambda b_, h, i: (b_, i, h)),
        out_shape=jax.ShapeDtypeStruct((B, S, DA_WIDTH), F32),
        scratch_shapes=[pltpu.VMEM((n_g, LANES, 2 * LANES), BF16),
                        pltpu.VMEM((n_g, 1, 2 * LANES), F32),
                        pltpu.VMEM((n_g, V_ROWS, 2 * LANES), F32),
                        pltpu.VMEM((2, n_g, tk, 2 * LANES), F32)],
        compiler_params=_cparams(("parallel", "parallel", "parallel")),
        name="diff_attn",
    )(lam_arr, q, k, vt, g_mat)


def _memkv_kernel(m_ref, w_ref, km_ref, vm_ref):
    kv = jnp.dot(m_ref[...].astype(BF16), w_ref[...], preferred_element_type=F32)
    km_ref[...] = kv[:, 0:XA_WIDTH].astype(BF16)
    vm_ref[...] = kv[:, XA_WIDTH:2 * XA_WIDTH].astype(BF16)


def _mem_kv(mem, w_mem_kv):
    B, M, D = mem.shape
    return pl.pallas_call(
        _memkv_kernel,
        grid=(B,),
        in_specs=[pl.BlockSpec((None, M, D), lambda b_: (b_, 0, 0)),
                  _const_spec((D, 2 * XA_WIDTH))],
        out_specs=[pl.BlockSpec((None, M, XA_WIDTH), lambda b_: (b_, 0, 0)),
                   pl.BlockSpec((None, M, XA_WIDTH), lambda b_: (b_, 0, 0))],
        out_shape=[jax.ShapeDtypeStruct((B, M, XA_WIDTH), BF16),
                   jax.ShapeDtypeStruct((B, M, XA_WIDTH), BF16)],
        compiler_params=_cparams(("parallel",)),
        name="mem_kv",
    )(mem, w_mem_kv)


def _silu(z):
    return z * (1.0 / (1.0 + jnp.exp(-z)))


def _sigmoid(z):
    return 1.0 / (1.0 + jnp.exp(-z))


def _merge_kernel(x_ref, pu_ref, pup_ref, pun_ref, o_ref, km_ref, vm_ref, wg_ref, pw_ref, ps_ref,
                  wa_ref, wb_ref, wc_ref, wo_ref, g_ref, b_ref, y_ref, ue_ref, *, seq_len, alpha):
    i = pl.program_id(1)
    n_i = pl.num_programs(1)
    tm = x_ref.shape[0]
    x = x_ref[...]
    xb = x.astype(BF16)

    u = pu_ref[...]
    ue_ref[0:POOL_HALO, :] = jnp.where(i > 0, pup_ref[...], 0.0)
    ue_ref[POOL_HALO:POOL_HALO + tm, :] = u
    ue_ref[POOL_HALO + tm:2 * POOL_HALO + tm, :] = jnp.where(i < n_i - 1, pun_ref[...], 0.0)
    t = lax.broadcasted_iota(jnp.int32, (tm, POOL_GROUP_WIDTH), 0) + i * tm
    ya_parts = []
    for g, w in enumerate(POOL_WINDOWS):
        cols = slice(g * POOL_GROUP_WIDTH, (g + 1) * POOL_GROUP_WIDTH)
        wsum = ue_ref[POOL_HALO - w // 2:POOL_HALO - w // 2 + tm, cols]
        for j in range(1, w):
            r = POOL_HALO - w // 2 + j
            wsum = wsum + ue_ref[r:r + tm, cols]
        cnt = jnp.minimum(t + (w // 2 - 1), seq_len - 1) - jnp.maximum(t - w // 2, 0) + 1
        d = (wsum / cnt.astype(F32) - u[:, cols]).astype(BF16)
        ya_parts.append(jnp.dot(d, pw_ref[g], preferred_element_type=F32))
    ya = jnp.concatenate(ya_parts, axis=1) * ps_ref[...]
    ya = ya * _silu(jnp.dot(xb, wg_ref[:, 0:512], preferred_element_type=F32))
    ya = jnp.dot(ya.astype(BF16), wa_ref[...], preferred_element_type=F32)

    yb = o_ref[...] * _silu(jnp.dot(xb, wg_ref[:, 512:1536], preferred_element_type=F32))
    yb = jnp.dot(yb.astype(BF16), wb_ref[...], preferred_element_type=F32)

    xq = jnp.dot(xb, wg_ref[:, 1536:2048], preferred_element_type=F32).astype(BF16)
    yc_parts = []
    for h in range(XA_HEADS):
        cols = slice(h * XA_HEAD_DIM, (h + 1) * XA_HEAD_DIM)
        sc = lax.dot_general(xq[:, cols], km_ref[:, cols], (((1,), (1,)), ((), ())),
                             preferred_element_type=F32) * (XA_HEAD_DIM ** -0.5)
        e = jnp.exp(sc - jnp.max(sc, axis=-1, keepdims=True))
        pm = (e / jnp.sum(e, axis=-1, keepdims=True)).astype(BF16)
        yc_parts.append(jnp.dot(pm, vm_ref[:, cols], preferred_element_type=F32))
    yc = jnp.concatenate(yc_parts, axis=1)
    yc = yc * _silu(jnp.dot(xb, wg_ref[:, 2048:2560], preferred_element_type=F32))
    yc = jnp.dot(yc.astype(BF16), wc_ref[...], preferred_element_type=F32)

    merged = _sigmoid(jnp.dot(xb, wg_ref[:, 2560:3584], preferred_element_type=F32)) * ya
    merged = merged + _sigmoid(jnp.dot(xb, wg_ref[:, 3584:4608], preferred_element_type=F32)) * yb
    merged = merged + _sigmoid(jnp.dot(xb, wg_ref[:, 4608:5632], preferred_element_type=F32)) * yc
    out = jnp.dot(merged.astype(BF16), wo_ref[...], preferred_element_type=F32)
    y_ref[...] = _ln_rows(alpha * x + out, g_ref[...], b_ref[...])


def _merge(x, pu, o, km, vm, w_g, pool_w, pool_scale, w_a, w_b, w_c, w_o, ln_g, ln_b, tm, alpha):
    B, S, D = x.shape
    hb = tm // POOL_HALO
    n_hb = S // POOL_HALO
    kern = functools.partial(_merge_kernel, seq_len=S, alpha=alpha)
    tile = lambda w: pl.BlockSpec((None, tm, w), lambda b_, i: (b_, i, 0))
    return pl.pallas_call(
        kern,
        grid=(B, S // tm),
        in_specs=[tile(D),
                  tile(POOL_WIDTH),
                  pl.BlockSpec((None, POOL_HALO, POOL_WIDTH), lambda b_, i: (b_, jnp.maximum(i * hb - 1, 0), 0)),
                  pl.BlockSpec((None, POOL_HALO, POOL_WIDTH),
                               lambda b_, i: (b_, jnp.minimum((i + 1) * hb, n_hb - 1), 0)),
                  tile(DA_WIDTH),
                  pl.BlockSpec((None, N_MEM, XA_WIDTH), lambda b_, i: (b_, 0, 0)),
                  pl.BlockSpec((None, N_MEM, XA_WIDTH), lambda b_, i: (b_, 0, 0)),
                  _const_spec(w_g.shape),
                  _const_spec(pool_w.shape),
                  _const_spec((1, POOL_WIDTH)),
                  _const_spec(w_a.shape), _const_spec(w_b.shape), _const_spec(w_c.shape), _const_spec(w_o.shape),
                  _const_spec((1, D)), _const_spec((1, D))],
        out_specs=tile(D),
        out_shape=jax.ShapeDtypeStruct((B, S, D), F32),
        scratch_shapes=[pltpu.VMEM((tm + 2 * POOL_HALO, POOL_WIDTH), F32)],
        compiler_params=_cparams(("parallel", "arbitrary")),
        name="merge",
    )(x, pu, pu, pu, o, km, vm, w_g, pool_w, pool_scale.reshape(1, POOL_WIDTH), w_a, w_b, w_c, w_o,
      ln_g.reshape(1, D), ln_b.reshape(1, D))


def _rope_tables(seq):
    pos = jnp.arange(seq, dtype=F32)
    inv = ROPE_THETA ** (-jnp.arange(0, ROT_DIM, 2, dtype=F32) / ROT_DIM)
    ang = pos[:, None] * inv[None, :]
    cos, sin = jnp.cos(ang), jnp.sin(ang)
    half = ROT_DIM // 2
    lane = jnp.arange(LANES) % DA_HEAD_DIM
    idx = lane % half
    c = jnp.where(lane[None, :] < ROT_DIM, cos[:, idx], 1.0)
    sa = jnp.where((lane[None, :] >= half) & (lane[None, :] < ROT_DIM), sin[:, idx], 0.0)
    sb = jnp.where(lane[None, :] < half, -sin[:, idx], 0.0)
    return c.astype(F32), sa.astype(F32), sb.astype(F32)


def _tiles(S):
    return dict(tm_ln=512, tm_proj=512, tq=512, tk=512, tm_merge=256)


def _trunk(x, mem, ln_in_g, ln_in_b, layers):
    B, S, _ = x.shape
    depth = len(layers)
    alpha = (2.0 * depth) ** 0.25
    t = _tiles(S)
    rope_c, rope_sa, rope_sb = _rope_tables(S)
    x = _layernorm(x, ln_in_g, ln_in_b, t["tm_ln"])
    for L in layers:
        pu, q, k, vt = _project(x, L["w_a"], rope_c, rope_sa, rope_sb, t["tm_proj"])
        o = _diff_attention(q, k, vt, L["lam"], L["subln"], t["tq"], t["tk"])
        km, vm = _mem_kv(mem, L["w_mem_kv"])
        x = _merge(x, pu, o, km, vm, L["w_g"], L["pool_w"], L["pool_scale"], L["w_br_a"], L["w_br_b"],
                   L["w_br_c"], L["w_out"], L["ln_g"], L["ln_b"], t["tm_merge"], alpha)
    return x


def _prepare_layers(w_in, w_mem_kv, pool_w, pool_scale, lam_q1, lam_k1, lam_q2, lam_k2, subln_g,
                    w_br_a, w_br_b, w_br_c, w_out, ln_g, ln_b):
    layers = []
    for i in range(w_in.shape[0]):
        lam_init = 0.8 - 0.6 * math.exp(-0.3 * i)
        lam = (jnp.exp(jnp.sum(lam_q1[i] * lam_k1[i])) - jnp.exp(jnp.sum(lam_q2[i] * lam_k2[i])) + lam_init)
        wi = w_in[i].astype(BF16)
        layers.append(dict(
            w_a=jnp.concatenate([wi[:, C_PU:C_PZ], wi[:, C_Q:C_AZ]], axis=1),
            w_g=jnp.concatenate([wi[:, C_PZ:C_Q], wi[:, C_AZ:C_END]], axis=1),
            w_mem_kv=w_mem_kv[i].astype(BF16),
            pool_w=pool_w[i].astype(BF16),
            pool_scale=pool_scale[i],
            lam=jnp.reshape(lam, (1,)).astype(F32),
            subln=jnp.broadcast_to((subln_g[i] * (1.0 - lam_init))[:, None], (DA_V_DIM, LANES)).astype(F32),
            w_br_a=w_br_a[i].astype(BF16), w_br_b=w_br_b[i].astype(BF16), w_br_c=w_br_c[i].astype(BF16),
            w_out=w_out[i].astype(BF16), ln_g=ln_g[i], ln_b=ln_b[i]))
    return layers


def kernel(x_prompt, x_sample, mem_prompt, mem_sample, ln_in_g, ln_in_b, w_in, w_mem_kv, pool_w, pool_scale,
           lam_q1, lam_k1, lam_q2, lam_k2, subln_g, w_br_a, w_br_b, w_br_c, w_out, ln_g, ln_b):
    layers = _prepare_layers(w_in, w_mem_kv, pool_w, pool_scale, lam_q1, lam_k1, lam_q2, lam_k2, subln_g,
                             w_br_a, w_br_b, w_br_c, w_out, ln_g, ln_b)
    y_prompt = _trunk(x_prompt, mem_prompt, ln_in_g, ln_in_b, layers)
    y_sample = _trunk(x_sample, mem_sample, ln_in_g, ln_in_b, layers)
    return (y_prompt, y_sample)
```

```python
import functools
import math

import jax
import jax.numpy as jnp
from jax import lax
from jax.experimental import pallas as pl
from jax.experimental.pallas import tpu as pltpu

F32 = jnp.float32
BF16 = jnp.bfloat16

D_MODEL = 1024
POOL_GROUPS = 4
POOL_GROUP_WIDTH = 128
POOL_WIDTH = POOL_GROUPS * POOL_GROUP_WIDTH
POOL_WINDOWS = (2, 4, 8, 16)
POOL_HALO = 8
DA_HEADS = 8
DA_HEAD_DIM = 64
DA_V_DIM = 2 * DA_HEAD_DIM
DA_WIDTH = DA_HEADS * DA_V_DIM
ROT_DIM = DA_HEAD_DIM // 4
ROPE_THETA = 500000.0
SUBLN_EPS = 1e-5
N_MEM = 256
XA_HEADS = 4
XA_HEAD_DIM = 128
XA_WIDTH = XA_HEADS * XA_HEAD_DIM
LN_EPS = 1e-5
LOG2E = 1.4426950408889634

C_PU, C_PZ, C_Q, C_K, C_V, C_AZ, C_XQ, C_XZ, C_GL, C_END = 0, 512, 1024, 2048, 3072, 4096, 5120, 5632, 6144, 9216

V_ROWS = DA_V_DIM + 16
LANES = 128
SUBLANES = 8
VMEM_LIMIT = 56 * 1024 * 1024


def _cparams(sem):
    return pltpu.CompilerParams(dimension_semantics=sem, vmem_limit_bytes=VMEM_LIMIT)


def _const_spec(shape):
    n = len(shape)
    return pl.BlockSpec(shape, lambda *_: (0,) * n, pipeline_mode=pl.Buffered(1))


def _ln_rows(x, g, b):
    mu = jnp.mean(x, axis=-1, keepdims=True)
    xc = x - mu
    var = jnp.mean(xc * xc, axis=-1, keepdims=True)
    return xc * lax.rsqrt(var + LN_EPS) * g + b


def _ln_kernel(x_ref, g_ref, b_ref, y_ref):
    y_ref[...] = _ln_rows(x_ref[...], g_ref[...], b_ref[...])


def _layernorm(x, g, b, tm):
    B, S, D = x.shape
    return pl.pallas_call(
        _ln_kernel,
        grid=(B, S // tm),
        in_specs=[pl.BlockSpec((None, tm, D), lambda b_, i: (b_, i, 0)),
                  _const_spec((1, D)), _const_spec((1, D))],
        out_specs=pl.BlockSpec((None, tm, D), lambda b_, i: (b_, i, 0)),
        out_shape=jax.ShapeDtypeStruct((B, S, D), F32),
        compiler_params=_cparams(("parallel", "parallel")),
        name="ln_in",
    )(x, g.reshape(1, D), b.reshape(1, D))


def _proj_kernel(x_ref, w_ref, c_ref, sa_ref, sb_ref, pu_ref, q_ref, k_ref, vt_ref):
    x = x_ref[...].astype(BF16)
    tm = x.shape[0]
    pu_ref[...] = jnp.dot(x, w_ref[:, 0:POOL_WIDTH], preferred_element_type=F32)
    qk = jnp.dot(x, w_ref[:, POOL_WIDTH:POOL_WIDTH + 2 * DA_WIDTH], preferred_element_type=F32)
    c = c_ref[...]
    sa = sa_ref[...]
    sb = sb_ref[...]
    q_scale = (DA_HEAD_DIM ** -0.5) * LOG2E
    for h in range(2 * DA_HEADS):
        blk = qk[:, h * LANES:(h + 1) * LANES]
        rot = blk * c + pltpu.roll(blk, 8, axis=1) * sa + pltpu.roll(blk, LANES - 8, axis=1) * sb
        if h < DA_HEADS:
            q_ref[:, h * LANES:(h + 1) * LANES] = (rot * q_scale).astype(BF16)
        else:
            hh = h - DA_HEADS
            k_ref[:, hh * LANES:(hh + 1) * LANES] = rot.astype(BF16)
    v = jnp.dot(x, w_ref[:, POOL_WIDTH + 2 * DA_WIDTH:POOL_WIDTH + 3 * DA_WIDTH], preferred_element_type=F32)
    ones = jnp.ones((V_ROWS - DA_V_DIM, tm), BF16)
    for h in range(DA_HEADS):
        vt_ref[h, 0:DA_V_DIM, :] = v[:, h * LANES:(h + 1) * LANES].T.astype(BF16)
        vt_ref[h, DA_V_DIM:V_ROWS, :] = ones


def _project(x, w_a, rope_c, rope_sa, rope_sb, tm):
    B, S, D = x.shape
    n_a = w_a.shape[1]
    return pl.pallas_call(
        _proj_kernel,
        grid=(B, S // tm),
        in_specs=[pl.BlockSpec((None, tm, D), lambda b_, i: (b_, i, 0)),
                  _const_spec((D, n_a)),
                  pl.BlockSpec((tm, LANES), lambda b_, i: (i, 0)),
                  pl.BlockSpec((tm, LANES), lambda b_, i: (i, 0)),
                  pl.BlockSpec((tm, LANES), lambda b_, i: (i, 0))],
        out_specs=[pl.BlockSpec((None, tm, POOL_WIDTH), lambda b_, i: (b_, i, 0)),
                   pl.BlockSpec((None, tm, DA_WIDTH), lambda b_, i: (b_, i, 0)),
                   pl.BlockSpec((None, tm, DA_WIDTH), lambda b_, i: (b_, i, 0)),
                   pl.BlockSpec((None, DA_HEADS, V_ROWS, tm), lambda b_, i: (b_, 0, 0, i))],
        out_shape=[jax.ShapeDtypeStruct((B, S, POOL_WIDTH), F32),
                   jax.ShapeDtypeStruct((B, S, DA_WIDTH), BF16),
                   jax.ShapeDtypeStruct((B, S, DA_WIDTH), BF16),
                   jax.ShapeDtypeStruct((B, DA_HEADS, V_ROWS, S), BF16)],
        compiler_params=_cparams(("parallel", "parallel")),
        name="proj_qkv",
    )(x, w_a, rope_c, rope_sa, rope_sb)


def _attn_kernel(lam_ref, q_ref, k_ref, vt_ref, g_ref, o_ref, wq_ref, m_ref, acc_ref, s_ref, tmax_ref,
                 *, tk, qb, unroll):
    tq = q_ref.shape[0]
    S = k_ref.shape[0]
    n_g = tq // qb
    lam = lam_ref[0]
    lane = lax.broadcasted_iota(jnp.int32, (LANES, qb), 0)

    for g in range(n_g):
        qt = q_ref[g * qb:(g + 1) * qb, :].astype(F32).T
        wq_ref[g] = jnp.concatenate([jnp.where(lane < DA_HEAD_DIM, qt, 0.0),
                                     jnp.where(lane >= DA_HEAD_DIM, qt, 0.0)], axis=1).astype(BF16)
        m_ref[g] = jnp.full((SUBLANES, 2 * qb), -1e30, F32)
        acc_ref[g] = jnp.zeros((V_ROWS, 2 * qb), F32)

    n_t = S // tk

    def scores(tile, buf):
        c0 = pl.multiple_of(tile * tk, tk)
        kt = k_ref[pl.ds(c0, tk), :]
        for g in range(n_g):
            s = jnp.dot(kt, wq_ref[g], preferred_element_type=F32)
            s_ref[buf, g] = s
            tmax_ref[buf, g] = jnp.broadcast_to(jnp.max(s, axis=0, keepdims=True), (SUBLANES, 2 * qb))

    def softmax_pv(tile, buf):
        c0 = pl.multiple_of(tile * tk, tk)
        vt = vt_ref[:, pl.ds(c0, tk)]
        for g in range(n_g):
            m_old = m_ref[g]
            m_new = jnp.maximum(m_old, tmax_ref[buf, g])
            m_ref[g] = m_new
            p = jnp.exp2(s_ref[buf, g] - m_new[0:1, :]).astype(BF16)
            pv = jnp.dot(vt, p, preferred_element_type=F32)
            acc_ref[g] = acc_ref[g] * jnp.exp2(m_old - m_new)[0:1, :] + pv

    def step(tile, parity, last=False):
        if not last:
            scores(tile + 1, 1 - parity)
        softmax_pv(tile, parity)

    assert unroll % 2 == 0
    scores(0, 0)
    n_loop = (n_t - 1) // unroll

    def kv_body(jj, carry):
        for u in range(unroll):
            step(jj * unroll + u, u % 2)
        return carry

    lax.fori_loop(0, n_loop, kv_body, 0)
    for t in range(n_loop * unroll, n_t):
        step(t, t % 2, last=(t == n_t - 1))

    for g in range(n_g):
        acc = acc_ref[g]
        o0 = acc[0:DA_V_DIM, 0:qb] / acc[DA_V_DIM:DA_V_DIM + 1, 0:qb]
        o1 = acc[0:DA_V_DIM, qb:2 * qb] / acc[DA_V_DIM:DA_V_DIM + 1, qb:2 * qb]
        o = o0 - lam * o1
        o = o * lax.rsqrt(jnp.mean(o * o, axis=0, keepdims=True) + SUBLN_EPS)
        o_ref[g * qb:(g + 1) * qb, :] = (o * g_ref[...]).T


def _diff_attention(q, k, vt, lam_arr, g_mat, tq, tk, unroll):
    B, S, _ = q.shape
    n_g = tq // LANES
    kern = functools.partial(_attn_kernel, tk=tk, qb=LANES, unroll=unroll)
    return pl.pallas_call(
        kern,
        grid=(B, DA_HEADS, S // tq),
        in_specs=[pl.BlockSpec(memory_space=pltpu.SMEM),
                  pl.BlockSpec((None, tq, LANES), lambda b_, h, i: (b_, i, h)),
                  pl.BlockSpec((None, S, LANES), lambda b_, h, i: (b_, 0, h)),
                  pl.BlockSpec((None, None, V_ROWS, S), lambda b_, h, i: (b_, h, 0, 0)),
                  pl.BlockSpec((DA_V_DIM, LANES), lambda b_, h, i: (0, 0))],
        out_specs=pl.BlockSpec((None, tq, LANES), lambda b_, h, i: (b_, i, h)),
        out_shape=jax.ShapeDtypeStruct((B, S, DA_WIDTH), F32),
        scratch_shapes=[pltpu.VMEM((n_g, LANES, 2 * LANES), BF16),
                        pltpu.VMEM((n_g, SUBLANES, 2 * LANES), F32),
                        pltpu.VMEM((n_g, V_ROWS, 2 * LANES), F32),
                        pltpu.VMEM((2, n_g, tk, 2 * LANES), F32),
                        pltpu.VMEM((2, n_g, SUBLANES, 2 * LANES), F32)],
        compiler_params=_cparams(("parallel", "parallel", "parallel")),
        name="diff_attn",
    )(lam_arr, q, k, vt, g_mat)


def _memkv_kernel(m_ref, w_ref, km_ref, vm_ref):
    kv = jnp.dot(m_ref[...].astype(BF16), w_ref[...], preferred_element_type=F32)
    km_ref[...] = kv[:, 0:XA_WIDTH].astype(BF16)
    vm_ref[...] = kv[:, XA_WIDTH:2 * XA_WIDTH].astype(BF16)


def _mem_kv(mem, w_mem_kv):
    B, M, D = mem.shape
    return pl.pallas_call(
        _memkv_kernel,
        grid=(B,),
        in_specs=[pl.BlockSpec((None, M, D), lambda b_: (b_, 0, 0)),
                  _const_spec((D, 2 * XA_WIDTH))],
        out_specs=[pl.BlockSpec((None, M, XA_WIDTH), lambda b_: (b_, 0, 0)),
                   pl.BlockSpec((None, M, XA_WIDTH), lambda b_: (b_, 0, 0))],
        out_shape=[jax.ShapeDtypeStruct((B, M, XA_WIDTH), BF16),
                   jax.ShapeDtypeStruct((B, M, XA_WIDTH), BF16)],
        compiler_params=_cparams(("parallel",)),
        name="mem_kv",
    )(mem, w_mem_kv)


def _silu(z):
    return z * (1.0 / (1.0 + jnp.exp(-z)))


def _sigmoid(z):
    return 1.0 / (1.0 + jnp.exp(-z))


def _merge_kernel(x_ref, pu_ref, pup_ref, pun_ref, o_ref, km_ref, vm_ref, wg_ref, pw_ref, ps_ref,
                  wa_ref, wb_ref, wc_ref, wo_ref, g_ref, b_ref, y_ref, ue_ref, *, seq_len, alpha):
    i = pl.program_id(1)
    n_i = pl.num_programs(1)
    tm = x_ref.shape[0]
    x = x_ref[...]
    xb = x.astype(BF16)

    u = pu_ref[...]
    ue_ref[0:POOL_HALO, :] = jnp.where(i > 0, pup_ref[...], 0.0)
    ue_ref[POOL_HALO:POOL_HALO + tm, :] = u
    ue_ref[POOL_HALO + tm:2 * POOL_HALO + tm, :] = jnp.where(i < n_i - 1, pun_ref[...], 0.0)
    t = lax.broadcasted_iota(jnp.int32, (tm, POOL_GROUP_WIDTH), 0) + i * tm
    ya_parts = []
    for g, w in enumerate(POOL_WINDOWS):
        cols = slice(g * POOL_GROUP_WIDTH, (g + 1) * POOL_GROUP_WIDTH)
        wsum = ue_ref[POOL_HALO - w // 2:POOL_HALO - w // 2 + tm, cols]
        for j in range(1, w):
            r = POOL_HALO - w // 2 + j
            wsum = wsum + ue_ref[r:r + tm, cols]
        cnt = jnp.minimum(t + (w // 2 - 1), seq_len - 1) - jnp.maximum(t - w // 2, 0) + 1
        d = (wsum / cnt.astype(F32) - u[:, cols]).astype(BF16)
        ya_parts.append(jnp.dot(d, pw_ref[g], preferred_element_type=F32))
    ya = jnp.concatenate(ya_parts, axis=1) * ps_ref[...]
    ya = ya * _silu(jnp.dot(xb, wg_ref[:, 0:512], preferred_element_type=F32))
    ya = jnp.dot(ya.astype(BF16), wa_ref[...], preferred_element_type=F32)

    yb = o_ref[...] * _silu(jnp.dot(xb, wg_ref[:, 512:1536], preferred_element_type=F32))
    yb = jnp.dot(yb.astype(BF16), wb_ref[...], preferred_element_type=F32)

    xq = jnp.dot(xb, wg_ref[:, 1536:2048], preferred_element_type=F32).astype(BF16)
    yc_parts = []
    for h in range(XA_HEADS):
        cols = slice(h * XA_HEAD_DIM, (h + 1) * XA_HEAD_DIM)
        sc = lax.dot_general(xq[:, cols], km_ref[:, cols], (((1,), (1,)), ((), ())),
                             preferred_element_type=F32) * (XA_HEAD_DIM ** -0.5)
        e = jnp.exp(sc - jnp.max(sc, axis=-1, keepdims=True))
        pm = (e / jnp.sum(e, axis=-1, keepdims=True)).astype(BF16)
        yc_parts.append(jnp.dot(pm, vm_ref[:, cols], preferred_element_type=F32))
    yc = jnp.concatenate(yc_parts, axis=1)
    yc = yc * _silu(jnp.dot(xb, wg_ref[:, 2048:2560], preferred_element_type=F32))
    yc = jnp.dot(yc.astype(BF16), wc_ref[...], preferred_element_type=F32)

    merged = _sigmoid(jnp.dot(xb, wg_ref[:, 2560:3584], preferred_element_type=F32)) * ya
    merged = merged + _sigmoid(jnp.dot(xb, wg_ref[:, 3584:4608], preferred_element_type=F32)) * yb
    merged = merged + _sigmoid(jnp.dot(xb, wg_ref[:, 4608:5632], preferred_element_type=F32)) * yc
    out = jnp.dot(merged.astype(BF16), wo_ref[...], preferred_element_type=F32)
    y_ref[...] = _ln_rows(alpha * x + out, g_ref[...], b_ref[...])


def _merge(x, pu, o, km, vm, w_g, pool_w, pool_scale, w_a, w_b, w_c, w_o, ln_g, ln_b, tm, alpha):
    B, S, D = x.shape
    hb = tm // POOL_HALO
    n_hb = S // POOL_HALO
    kern = functools.partial(_merge_kernel, seq_len=S, alpha=alpha)
    tile = lambda w: pl.BlockSpec((None, tm, w), lambda b_, i: (b_, i, 0))
    return pl.pallas_call(
        kern,
        grid=(B, S // tm),
        in_specs=[tile(D),
                  tile(POOL_WIDTH),
                  pl.BlockSpec((None, POOL_HALO, POOL_WIDTH), lambda b_, i: (b_, jnp.maximum(i * hb - 1, 0), 0)),
                  pl.BlockSpec((None, POOL_HALO, POOL_WIDTH),
                               lambda b_, i: (b_, jnp.minimum((i + 1) * hb, n_hb - 1), 0)),
                  tile(DA_WIDTH),
                  pl.BlockSpec((None, N_MEM, XA_WIDTH), lambda b_, i: (b_, 0, 0)),
                  pl.BlockSpec((None, N_MEM, XA_WIDTH), lambda b_, i: (b_, 0, 0)),
                  _const_spec(w_g.shape),
                  _const_spec(pool_w.shape),
                  _const_spec((1, POOL_WIDTH)),
                  _const_spec(w_a.shape), _const_spec(w_b.shape), _const_spec(w_c.shape), _const_spec(w_o.shape),
                  _const_spec((1, D)), _const_spec((1, D))],
        out_specs=tile(D),
        out_shape=jax.ShapeDtypeStruct((B, S, D), F32),
        scratch_shapes=[pltpu.VMEM((tm + 2 * POOL_HALO, POOL_WIDTH), F32)],
        compiler_params=_cparams(("parallel", "arbitrary")),
        name="merge",
    )(x, pu, pu, pu, o, km, vm, w_g, pool_w, pool_scale.reshape(1, POOL_WIDTH), w_a, w_b, w_c, w_o,
      ln_g.reshape(1, D), ln_b.reshape(1, D))


def _rope_tables(seq):
    pos = jnp.arange(seq, dtype=F32)
    inv = ROPE_THETA ** (-jnp.arange(0, ROT_DIM, 2, dtype=F32) / ROT_DIM)
    ang = pos[:, None] * inv[None, :]
    cos, sin = jnp.cos(ang), jnp.sin(ang)
    half = ROT_DIM // 2
    lane = jnp.arange(LANES) % DA_HEAD_DIM
    idx = lane % half
    c = jnp.where(lane[None, :] < ROT_DIM, cos[:, idx], 1.0)
    sa = jnp.where((lane[None, :] >= half) & (lane[None, :] < ROT_DIM), sin[:, idx], 0.0)
    sb = jnp.where(lane[None, :] < half, -sin[:, idx], 0.0)
    return c.astype(F32), sa.astype(F32), sb.astype(F32)


def _tiles(S):
    if S >= 8192:
        attn = dict(tq=512, tk=512, unroll=4)
    else:
        attn = dict(tq=1024, tk=512, unroll=2)
    return dict(tm_ln=512, tm_proj=512, tm_merge=512, **attn)


def _trunk(x, mem, ln_in_g, ln_in_b, layers):
    B, S, _ = x.shape
    depth = len(layers)
    alpha = (2.0 * depth) ** 0.25
    t = _tiles(S)
    rope_c, rope_sa, rope_sb = _rope_tables(S)
    x = _layernorm(x, ln_in_g, ln_in_b, t["tm_ln"])
    for L in layers:
        pu, q, k, vt = _project(x, L["w_a"], rope_c, rope_sa, rope_sb, t["tm_proj"])
        o = _diff_attention(q, k, vt, L["lam"], L["subln"], t["tq"], t["tk"], t["unroll"])
        km, vm = _mem_kv(mem, L["w_mem_kv"])
        x = _merge(x, pu, o, km, vm, L["w_g"], L["pool_w"], L["pool_scale"], L["w_br_a"], L["w_br_b"],
                   L["w_br_c"], L["w_out"], L["ln_g"], L["ln_b"], t["tm_merge"], alpha)
    return x


def _prepare_layers(w_in, w_mem_kv, pool_w, pool_scale, lam_q1, lam_k1, lam_q2, lam_k2, subln_g,
                    w_br_a, w_br_b, w_br_c, w_out, ln_g, ln_b):
    layers = []
    for i in range(w_in.shape[0]):
        lam_init = 0.8 - 0.6 * math.exp(-0.3 * i)
        lam = (jnp.exp(jnp.sum(lam_q1[i] * lam_k1[i])) - jnp.exp(jnp.sum(lam_q2[i] * lam_k2[i])) + lam_init)
        wi = w_in[i].astype(BF16)
        layers.append(dict(
            w_a=jnp.concatenate([wi[:, C_PU:C_PZ], wi[:, C_Q:C_AZ]], axis=1),
            w_g=jnp.concatenate([wi[:, C_PZ:C_Q], wi[:, C_AZ:C_END]], axis=1),
            w_mem_kv=w_mem_kv[i].astype(BF16),
            pool_w=pool_w[i].astype(BF16),
            pool_scale=pool_scale[i],
            lam=jnp.reshape(lam, (1,)).astype(F32),
            subln=jnp.broadcast_to((subln_g[i] * (1.0 - lam_init))[:, None], (DA_V_DIM, LANES)).astype(F32),
            w_br_a=w_br_a[i].astype(BF16), w_br_b=w_br_b[i].astype(BF16), w_br_c=w_br_c[i].astype(BF16),
            w_out=w_out[i].astype(BF16), ln_g=ln_g[i], ln_b=ln_b[i]))
    return layers


def kernel(x_prompt, x_sample, mem_prompt, mem_sample, ln_in_g, ln_in_b, w_in, w_mem_kv, pool_w, pool_scale,
           lam_q1, lam_k1, lam_q2, lam_k2, subln_g, w_br_a, w_br_b, w_br_c, w_out, ln_g, ln_b):
    layers = _prepare_layers(w_in, w_mem_kv, pool_w, pool_scale, lam_q1, lam_k1, lam_q2, lam_k2, subln_g,
                             w_br_a, w_br_b, w_br_c, w_out, ln_g, ln_b)
    y_prompt = _trunk(x_prompt, mem_prompt, ln_in_g, ln_in_b, layers)
    y_sample = _trunk(x_sample, mem_sample, ln_in_g, ln_in_b, layers)
    return (y_prompt, y_sample)
```

```python
import functools
import math

import jax
import jax.numpy as jnp
from jax import lax
from jax.experimental import pallas as pl
from jax.experimental.pallas import tpu as pltpu

F32 = jnp.float32
BF16 = jnp.bfloat16

D_MODEL = 1024
POOL_GROUPS = 4
POOL_GROUP_WIDTH = 128
POOL_WIDTH = POOL_GROUPS * POOL_GROUP_WIDTH
POOL_WINDOWS = (2, 4, 8, 16)
POOL_HALO = 8
DA_HEADS = 8
DA_HEAD_DIM = 64
DA_V_DIM = 2 * DA_HEAD_DIM
DA_WIDTH = DA_HEADS * DA_V_DIM
ROT_DIM = DA_HEAD_DIM // 4
ROPE_THETA = 500000.0
SUBLN_EPS = 1e-5
N_MEM = 256
XA_HEADS = 4
XA_HEAD_DIM = 128
XA_WIDTH = XA_HEADS * XA_HEAD_DIM
LN_EPS = 1e-5
LOG2E = 1.4426950408889634

C_PU, C_PZ, C_Q, C_K, C_V, C_AZ, C_XQ, C_XZ, C_GL, C_END = 0, 512, 1024, 2048, 3072, 4096, 5120, 5632, 6144, 9216

V_ROWS = DA_V_DIM + 16
LANES = 128
SUBLANES = 8
VMEM_LIMIT = 56 * 1024 * 1024


def _cparams(sem):
    return pltpu.CompilerParams(dimension_semantics=sem, vmem_limit_bytes=VMEM_LIMIT)


def _const_spec(shape):
    n = len(shape)
    return pl.BlockSpec(shape, lambda *_: (0,) * n, pipeline_mode=pl.Buffered(1))


def _ln_rows(x, g, b):
    mu = jnp.mean(x, axis=-1, keepdims=True)
    xc = x - mu
    var = jnp.mean(xc * xc, axis=-1, keepdims=True)
    return xc * lax.rsqrt(var + LN_EPS) * g + b


def _ln_kernel(x_ref, g_ref, b_ref, y_ref):
    y_ref[...] = _ln_rows(x_ref[...], g_ref[...], b_ref[...])


def _layernorm(x, g, b, tm):
    B, S, D = x.shape
    return pl.pallas_call(
        _ln_kernel,
        grid=(B, S // tm),
        in_specs=[pl.BlockSpec((None, tm, D), lambda b_, i: (b_, i, 0)),
                  _const_spec((1, D)), _const_spec((1, D))],
        out_specs=pl.BlockSpec((None, tm, D), lambda b_, i: (b_, i, 0)),
        out_shape=jax.ShapeDtypeStruct((B, S, D), F32),
        compiler_params=_cparams(("parallel", "parallel")),
        name="ln_in",
    )(x, g.reshape(1, D), b.reshape(1, D))


def _proj_kernel(x_ref, w_ref, c_ref, sa_ref, sb_ref, pu_ref, q_ref, k_ref, vt_ref):
    x = x_ref[...].astype(BF16)
    tm = x.shape[0]
    pu_ref[...] = jnp.dot(x, w_ref[:, 0:POOL_WIDTH], preferred_element_type=F32)
    qk = jnp.dot(x, w_ref[:, POOL_WIDTH:POOL_WIDTH + 2 * DA_WIDTH], preferred_element_type=F32)
    c = c_ref[...]
    sa = sa_ref[...]
    sb = sb_ref[...]
    q_scale = (DA_HEAD_DIM ** -0.5) * LOG2E
    for h in range(2 * DA_HEADS):
        blk = qk[:, h * LANES:(h + 1) * LANES]
        rot = blk * c + pltpu.roll(blk, 8, axis=1) * sa + pltpu.roll(blk, LANES - 8, axis=1) * sb
        if h < DA_HEADS:
            q_ref[:, h * LANES:(h + 1) * LANES] = (rot * q_scale).astype(BF16)
        else:
            hh = h - DA_HEADS
            k_ref[:, hh * LANES:(hh + 1) * LANES] = rot.astype(BF16)
    v = jnp.dot(x, w_ref[:, POOL_WIDTH + 2 * DA_WIDTH:POOL_WIDTH + 3 * DA_WIDTH], preferred_element_type=F32)
    ones = jnp.ones((V_ROWS - DA_V_DIM, tm), BF16)
    for h in range(DA_HEADS):
        vt_ref[h, 0:DA_V_DIM, :] = v[:, h * LANES:(h + 1) * LANES].T.astype(BF16)
        vt_ref[h, DA_V_DIM:V_ROWS, :] = ones


def _project(x, w_a, rope_c, rope_sa, rope_sb, tm):
    B, S, D = x.shape
    n_a = w_a.shape[1]
    return pl.pallas_call(
        _proj_kernel,
        grid=(B, S // tm),
        in_specs=[pl.BlockSpec((None, tm, D), lambda b_, i: (b_, i, 0)),
                  _const_spec((D, n_a)),
                  pl.BlockSpec((tm, LANES), lambda b_, i: (i, 0)),
                  pl.BlockSpec((tm, LANES), lambda b_, i: (i, 0)),
                  pl.BlockSpec((tm, LANES), lambda b_, i: (i, 0))],
        out_specs=[pl.BlockSpec((None, tm, POOL_WIDTH), lambda b_, i: (b_, i, 0)),
                   pl.BlockSpec((None, tm, DA_WIDTH), lambda b_, i: (b_, i, 0)),
                   pl.BlockSpec((None, tm, DA_WIDTH), lambda b_, i: (b_, i, 0)),
                   pl.BlockSpec((None, DA_HEADS, V_ROWS, tm), lambda b_, i: (b_, 0, 0, i))],
        out_shape=[jax.ShapeDtypeStruct((B, S, POOL_WIDTH), F32),
                   jax.ShapeDtypeStruct((B, S, DA_WIDTH), BF16),
                   jax.ShapeDtypeStruct((B, S, DA_WIDTH), BF16),
                   jax.ShapeDtypeStruct((B, DA_HEADS, V_ROWS, S), BF16)],
        compiler_params=_cparams(("parallel", "parallel")),
        name="proj_qkv",
    )(x, w_a, rope_c, rope_sa, rope_sb)


def _attn_kernel(lam_ref, q_ref, qn_ref, k_ref, vt_ref, g_ref, o_ref, wq_ref, wqn_ref, m_ref, acc_ref, s_ref,
                 tmax_ref, *, tk, qb, unroll):
    tq = q_ref.shape[0]
    S = k_ref.shape[0]
    n_g = tq // qb
    n_t = S // tk
    lam = lam_ref[0]
    lane = lax.broadcasted_iota(jnp.int32, (LANES, qb), 0)

    def build_query_weights(src_ref, dst_ref):
        for g in range(n_g):
            qt = src_ref[g * qb:(g + 1) * qb, :].astype(F32).T
            dst_ref[g] = jnp.concatenate([jnp.where(lane < DA_HEAD_DIM, qt, 0.0),
                                          jnp.where(lane >= DA_HEAD_DIM, qt, 0.0)], axis=1).astype(BF16)

    def scores(tile, buf, w_ref=wq_ref):
        c0 = pl.multiple_of(tile * tk, tk)
        kt = k_ref[pl.ds(c0, tk), :]
        for g in range(n_g):
            s = jnp.dot(kt, w_ref[g], preferred_element_type=F32)
            s_ref[buf, g] = s
            tmax_ref[buf, g] = jnp.broadcast_to(jnp.max(s, axis=0, keepdims=True), (SUBLANES, 2 * qb))

    def softmax_pv(tile, buf):
        c0 = pl.multiple_of(tile * tk, tk)
        vt = vt_ref[:, pl.ds(c0, tk)]
        for g in range(n_g):
            m_old = m_ref[g]
            m_new = jnp.maximum(m_old, tmax_ref[buf, g])
            m_ref[g] = m_new
            p = jnp.exp2(s_ref[buf, g] - m_new[0:1, :]).astype(BF16)
            pv = jnp.dot(vt, p, preferred_element_type=F32)
            acc_ref[g] = acc_ref[g] * jnp.exp2(m_old - m_new)[0:1, :] + pv

    def step(tile, parity):
        scores(tile + 1, 1 - parity)
        softmax_pv(tile, parity)

    assert unroll % 2 == 0 and n_t % 2 == 0
    first_block = pl.program_id(2) == 0

    @pl.when(first_block)
    def _():
        build_query_weights(q_ref, wq_ref)
        scores(0, 0)

    @pl.when(jnp.logical_not(first_block))
    def _():
        wq_ref[...] = wqn_ref[...]

    for g in range(n_g):
        m_ref[g] = jnp.full((SUBLANES, 2 * qb), -1e30, F32)
        acc_ref[g] = jnp.zeros((V_ROWS, 2 * qb), F32)

    n_loop = (n_t - 1) // unroll

    def kv_body(jj, carry):
        for u in range(unroll):
            step(jj * unroll + u, u % 2)
        return carry

    lax.fori_loop(0, n_loop, kv_body, 0)
    for t in range(n_loop * unroll, n_t - 1):
        step(t, t % 2)
    build_query_weights(qn_ref, wqn_ref)
    scores(0, 0, wqn_ref)
    softmax_pv(n_t - 1, 1)

    for g in range(n_g):
        acc = acc_ref[g]
        o0 = acc[0:DA_V_DIM, 0:qb] / acc[DA_V_DIM:DA_V_DIM + 1, 0:qb]
        o1 = acc[0:DA_V_DIM, qb:2 * qb] / acc[DA_V_DIM:DA_V_DIM + 1, qb:2 * qb]
        o = o0 - lam * o1
        o = o * lax.rsqrt(jnp.mean(o * o, axis=0, keepdims=True) + SUBLN_EPS)
        o_ref[g * qb:(g + 1) * qb, :] = (o * g_ref[...]).T


def _diff_attention(q, k, vt, lam_arr, g_mat, tq, tk, unroll):
    B, S, _ = q.shape
    n_g = tq // LANES
    n_q = S // tq
    kern = functools.partial(_attn_kernel, tk=tk, qb=LANES, unroll=unroll)
    return pl.pallas_call(
        kern,
        grid=(B, DA_HEADS, n_q),
        in_specs=[pl.BlockSpec(memory_space=pltpu.SMEM),
                  pl.BlockSpec((None, tq, LANES), lambda b_, h, i: (b_, i, h)),
                  pl.BlockSpec((None, tq, LANES), lambda b_, h, i: (b_, jnp.minimum(i + 1, n_q - 1), h)),
                  pl.BlockSpec((None, S, LANES), lambda b_, h, i: (b_, 0, h)),
                  pl.BlockSpec((None, None, V_ROWS, S), lambda b_, h, i: (b_, h, 0, 0)),
                  pl.BlockSpec((DA_V_DIM, LANES), lambda b_, h, i: (0, 0))],
        out_specs=pl.BlockSpec((None, tq, LANES), lambda b_, h, i: (b_, i, h)),
        out_shape=jax.ShapeDtypeStruct((B, S, DA_WIDTH), F32),
        scratch_shapes=[pltpu.VMEM((n_g, LANES, 2 * LANES), BF16),
                        pltpu.VMEM((n_g, LANES, 2 * LANES), BF16),
                        pltpu.VMEM((n_g, SUBLANES, 2 * LANES), F32),
                        pltpu.VMEM((n_g, V_ROWS, 2 * LANES), F32),
                        pltpu.VMEM((2, n_g, tk, 2 * LANES), F32),
                        pltpu.VMEM((2, n_g, SUBLANES, 2 * LANES), F32)],
        compiler_params=_cparams(("parallel", "parallel", "arbitrary")),
        name="diff_attn",
    )(lam_arr, q, q, k, vt, g_mat)


def _memkv_kernel(m_ref, w_ref, km_ref, vm_ref):
    kv = jnp.dot(m_ref[...].astype(BF16), w_ref[...], preferred_element_type=F32)
    km_ref[...] = kv[:, 0:XA_WIDTH].astype(BF16)
    vm_ref[...] = kv[:, XA_WIDTH:2 * XA_WIDTH].astype(BF16)


def _mem_kv(mem, w_mem_kv):
    B, M, D = mem.shape
    return pl.pallas_call(
        _memkv_kernel,
        grid=(B,),
        in_specs=[pl.BlockSpec((None, M, D), lambda b_: (b_, 0, 0)),
                  _const_spec((D, 2 * XA_WIDTH))],
        out_specs=[pl.BlockSpec((None, M, XA_WIDTH), lambda b_: (b_, 0, 0)),
                   pl.BlockSpec((None, M, XA_WIDTH), lambda b_: (b_, 0, 0))],
        out_shape=[jax.ShapeDtypeStruct((B, M, XA_WIDTH), BF16),
                   jax.ShapeDtypeStruct((B, M, XA_WIDTH), BF16)],
        compiler_params=_cparams(("parallel",)),
        name="mem_kv",
    )(mem, w_mem_kv)


def _silu(z):
    return z * (1.0 / (1.0 + jnp.exp(-z)))


def _sigmoid(z):
    return 1.0 / (1.0 + jnp.exp(-z))


def _merge_kernel(x_ref, pu_ref, pup_ref, pun_ref, o_ref, km_ref, vm_ref, wg_ref, pw_ref, ps_ref,
                  wa_ref, wb_ref, wc_ref, wo_ref, g_ref, b_ref, y_ref, ue_ref, *, seq_len, alpha):
    i = pl.program_id(1)
    n_i = pl.num_programs(1)
    tm = x_ref.shape[0]
    x = x_ref[...]
    xb = x.astype(BF16)

    def gate(c0, c1):
        return jnp.dot(xb, wg_ref[:, c0:c1], preferred_element_type=F32)

    xq = gate(1536, 2048).astype(BF16)
    sc = []
    for h in range(XA_HEADS):
        cols = slice(h * XA_HEAD_DIM, (h + 1) * XA_HEAD_DIM)
        sc.append(lax.dot_general(xq[:, cols], km_ref[:, cols], (((1,), (1,)), ((), ())),
                                  preferred_element_type=F32) * (XA_HEAD_DIM ** -0.5))

    yb = o_ref[...] * _silu(gate(512, 1536))
    yb = jnp.dot(yb.astype(BF16), wb_ref[...], preferred_element_type=F32)
    z_pool = gate(0, 512)
    z_mem = gate(2048, 2560)

    yc_parts = []
    for h in range(XA_HEADS):
        cols = slice(h * XA_HEAD_DIM, (h + 1) * XA_HEAD_DIM)
        e = jnp.exp(sc[h] - jnp.max(sc[h], axis=-1, keepdims=True))
        pm = (e / jnp.sum(e, axis=-1, keepdims=True)).astype(BF16)
        yc_parts.append(jnp.dot(pm, vm_ref[:, cols], preferred_element_type=F32))
    yc = jnp.concatenate(yc_parts, axis=1) * _silu(z_mem)
    g_a = _sigmoid(gate(2560, 3584))
    yc = jnp.dot(yc.astype(BF16), wc_ref[...], preferred_element_type=F32)

    u = pu_ref[...]
    ue_ref[0:POOL_HALO, :] = jnp.where(i > 0, pup_ref[...], 0.0)
    ue_ref[POOL_HALO:POOL_HALO + tm, :] = u
    ue_ref[POOL_HALO + tm:2 * POOL_HALO + tm, :] = jnp.where(i < n_i - 1, pun_ref[...], 0.0)
    t = lax.broadcasted_iota(jnp.int32, (tm, POOL_GROUP_WIDTH), 0) + i * tm
    ya_parts = []
    for g, w in enumerate(POOL_WINDOWS):
        cols = slice(g * POOL_GROUP_WIDTH, (g + 1) * POOL_GROUP_WIDTH)
        wsum = ue_ref[POOL_HALO - w // 2:POOL_HALO - w // 2 + tm, cols]
        for j in range(1, w):
            r = POOL_HALO - w // 2 + j
            wsum = wsum + ue_ref[r:r + tm, cols]
        cnt = jnp.minimum(t + (w // 2 - 1), seq_len - 1) - jnp.maximum(t - w // 2, 0) + 1
        d = (wsum / cnt.astype(F32) - u[:, cols]).astype(BF16)
        ya_parts.append(jnp.dot(d, pw_ref[g], preferred_element_type=F32))
    ya = jnp.concatenate(ya_parts, axis=1) * ps_ref[...] * _silu(z_pool)
    g_b = _sigmoid(gate(3584, 4608))
    ya = jnp.dot(ya.astype(BF16), wa_ref[...], preferred_element_type=F32)

    merged = g_a * ya + g_b * yb + _sigmoid(gate(4608, 5632)) * yc
    out = jnp.dot(merged.astype(BF16), wo_ref[...], preferred_element_type=F32)
    y_ref[...] = _ln_rows(alpha * x + out, g_ref[...], b_ref[...])


def _merge(x, pu, o, km, vm, w_g, pool_w, pool_scale, w_a, w_b, w_c, w_o, ln_g, ln_b, tm, alpha):
    B, S, D = x.shape
    hb = tm // POOL_HALO
    n_hb = S // POOL_HALO
    kern = functools.partial(_merge_kernel, seq_len=S, alpha=alpha)
    tile = lambda w: pl.BlockSpec((None, tm, w), lambda b_, i: (b_, i, 0))
    return pl.pallas_call(
        kern,
        grid=(B, S // tm),
        in_specs=[tile(D),
                  tile(POOL_WIDTH),
                  pl.BlockSpec((None, POOL_HALO, POOL_WIDTH), lambda b_, i: (b_, jnp.maximum(i * hb - 1, 0), 0)),
                  pl.BlockSpec((None, POOL_HALO, POOL_WIDTH),
                               lambda b_, i: (b_, jnp.minimum((i + 1) * hb, n_hb - 1), 0)),
                  tile(DA_WIDTH),
                  pl.BlockSpec((None, N_MEM, XA_WIDTH), lambda b_, i: (b_, 0, 0)),
                  pl.BlockSpec((None, N_MEM, XA_WIDTH), lambda b_, i: (b_, 0, 0)),
                  _const_spec(w_g.shape),
                  _const_spec(pool_w.shape),
                  _const_spec((1, POOL_WIDTH)),
                  _const_spec(w_a.shape), _const_spec(w_b.shape), _const_spec(w_c.shape), _const_spec(w_o.shape),
                  _const_spec((1, D)), _const_spec((1, D))],
        out_specs=tile(D),
        out_shape=jax.ShapeDtypeStruct((B, S, D), F32),
        scratch_shapes=[pltpu.VMEM((tm + 2 * POOL_HALO, POOL_WIDTH), F32)],
        compiler_params=_cparams(("parallel", "arbitrary")),
        name="merge",
    )(x, pu, pu, pu, o, km, vm, w_g, pool_w, pool_scale.reshape(1, POOL_WIDTH), w_a, w_b, w_c, w_o,
      ln_g.reshape(1, D), ln_b.reshape(1, D))


def _rope_tables(seq):
    pos = jnp.arange(seq, dtype=F32)
    inv = ROPE_THETA ** (-jnp.arange(0, ROT_DIM, 2, dtype=F32) / ROT_DIM)
    ang = pos[:, None] * inv[None, :]
    cos, sin = jnp.cos(ang), jnp.sin(ang)
    half = ROT_DIM // 2
    lane = jnp.arange(LANES) % DA_HEAD_DIM
    idx = lane % half
    c = jnp.where(lane[None, :] < ROT_DIM, cos[:, idx], 1.0)
    sa = jnp.where((lane[None, :] >= half) & (lane[None, :] < ROT_DIM), sin[:, idx], 0.0)
    sb = jnp.where(lane[None, :] < half, -sin[:, idx], 0.0)
    return c.astype(F32), sa.astype(F32), sb.astype(F32)


def _tiles(S):
    if S >= 8192:
        attn = dict(tq=512, tk=512, unroll=4)
    else:
        attn = dict(tq=1024, tk=256, unroll=4)
    return dict(tm_ln=512, tm_proj=512, tm_merge=512, **attn)


def _trunk(x, mem, ln_in_g, ln_in_b, layers):
    B, S, _ = x.shape
    depth = len(layers)
    alpha = (2.0 * depth) ** 0.25
    t = _tiles(S)
    rope_c, rope_sa, rope_sb = _rope_tables(S)
    x = _layernorm(x, ln_in_g, ln_in_b, t["tm_ln"])
    for L in layers:
        pu, q, k, vt = _project(x, L["w_a"], rope_c, rope_sa, rope_sb, t["tm_proj"])
        o = _diff_attention(q, k, vt, L["lam"], L["subln"], t["tq"], t["tk"], t["unroll"])
        km, vm = _mem_kv(mem, L["w_mem_kv"])
        x = _merge(x, pu, o, km, vm, L["w_g"], L["pool_w"], L["pool_scale"], L["w_br_a"], L["w_br_b"],
                   L["w_br_c"], L["w_out"], L["ln_g"], L["ln_b"], t["tm_merge"], alpha)
    return x


def _prepare_layers(w_in, w_mem_kv, pool_w, pool_scale, lam_q1, lam_k1, lam_q2, lam_k2, subln_g,
                    w_br_a, w_br_b, w_br_c, w_out, ln_g, ln_b):
    layers = []
    for i in range(w_in.shape[0]):
        lam_init = 0.8 - 0.6 * math.exp(-0.3 * i)
        lam = (jnp.exp(jnp.sum(lam_q1[i] * lam_k1[i])) - jnp.exp(jnp.sum(lam_q2[i] * lam_k2[i])) + lam_init)
        wi = w_in[i].astype(BF16)
        layers.append(dict(
            w_a=jnp.concatenate([wi[:, C_PU:C_PZ], wi[:, C_Q:C_AZ]], axis=1),
            w_g=jnp.concatenate([wi[:, C_PZ:C_Q], wi[:, C_AZ:C_END]], axis=1),
            w_mem_kv=w_mem_kv[i].astype(BF16),
            pool_w=pool_w[i].astype(BF16),
            pool_scale=pool_scale[i],
            lam=jnp.reshape(lam, (1,)).astype(F32),
            subln=jnp.broadcast_to((subln_g[i] * (1.0 - lam_init))[:, None], (DA_V_DIM, LANES)).astype(F32),
            w_br_a=w_br_a[i].astype(BF16), w_br_b=w_br_b[i].astype(BF16), w_br_c=w_br_c[i].astype(BF16),
            w_out=w_out[i].astype(BF16), ln_g=ln_g[i], ln_b=ln_b[i]))
    return layers


def kernel(x_prompt, x_sample, mem_prompt, mem_sample, ln_in_g, ln_in_b, w_in, w_mem_kv, pool_w, pool_scale,
           lam_q1, lam_k1, lam_q2, lam_k2, subln_g, w_br_a, w_br_b, w_br_c, w_out, ln_g, ln_b):
    layers = _prepare_layers(w_in, w_mem_kv, pool_w, pool_scale, lam_q1, lam_k1, lam_q2, lam_k2, subln_g,
                             w_br_a, w_br_b, w_br_c, w_out, ln_g, ln_b)
    y_prompt = _trunk(x_prompt, mem_prompt, ln_in_g, ln_in_b, layers)
    y_sample = _trunk(x_sample, mem_sample, ln_in_g, ln_in_b, layers)
    return (y_prompt, y_sample)
```

```python
import functools
import math

import jax
import jax.numpy as jnp
from jax import lax
from jax.experimental import pallas as pl
from jax.experimental.pallas import tpu as pltpu

F32 = jnp.float32
BF16 = jnp.bfloat16

D_MODEL = 1024
POOL_GROUPS = 4
POOL_GROUP_WIDTH = 128
POOL_WIDTH = POOL_GROUPS * POOL_GROUP_WIDTH
POOL_WINDOWS = (2, 4, 8, 16)
POOL_HALO = 8
DA_HEADS = 8
DA_HEAD_DIM = 64
DA_V_DIM = 2 * DA_HEAD_DIM
DA_WIDTH = DA_HEADS * DA_V_DIM
ROT_DIM = DA_HEAD_DIM // 4
ROPE_THETA = 500000.0
SUBLN_EPS = 1e-5
N_MEM = 256
XA_HEADS = 4
XA_HEAD_DIM = 128
XA_WIDTH = XA_HEADS * XA_HEAD_DIM
LN_EPS = 1e-5
LOG2E = 1.4426950408889634

C_PU, C_PZ, C_Q, C_K, C_V, C_AZ, C_XQ, C_XZ, C_GL, C_END = 0, 512, 1024, 2048, 3072, 4096, 5120, 5632, 6144, 9216

V_ROWS = DA_V_DIM + 16
LANES = 128
SUBLANES = 8
VMEM_LIMIT = 56 * 1024 * 1024


def _cparams(sem):
    return pltpu.CompilerParams(dimension_semantics=sem, vmem_limit_bytes=VMEM_LIMIT)


def _const_spec(shape):
    n = len(shape)
    return pl.BlockSpec(shape, lambda *_: (0,) * n, pipeline_mode=pl.Buffered(1))


def _ln_rows(x, g, b):
    mu = jnp.mean(x, axis=-1, keepdims=True)
    xc = x - mu
    var = jnp.mean(xc * xc, axis=-1, keepdims=True)
    return xc * lax.rsqrt(var + LN_EPS) * g + b


def _ln_kernel(x_ref, g_ref, b_ref, y_ref):
    y_ref[...] = _ln_rows(x_ref[...], g_ref[...], b_ref[...])


def _layernorm(x, g, b, tm):
    B, S, D = x.shape
    return pl.pallas_call(
        _ln_kernel,
        grid=(B, S // tm),
        in_specs=[pl.BlockSpec((None, tm, D), lambda b_, i: (b_, i, 0)),
                  _const_spec((1, D)), _const_spec((1, D))],
        out_specs=pl.BlockSpec((None, tm, D), lambda b_, i: (b_, i, 0)),
        out_shape=jax.ShapeDtypeStruct((B, S, D), F32),
        compiler_params=_cparams(("parallel", "parallel")),
        name="ln_in",
    )(x, g.reshape(1, D), b.reshape(1, D))


def _proj_kernel(x_ref, w_ref, c_ref, sa_ref, sb_ref, pu_ref, q_ref, k_ref, vt_ref):
    x = x_ref[...].astype(BF16)
    tm = x.shape[0]
    pu_ref[...] = jnp.dot(x, w_ref[:, 0:POOL_WIDTH], preferred_element_type=F32)
    qk = jnp.dot(x, w_ref[:, POOL_WIDTH:POOL_WIDTH + 2 * DA_WIDTH], preferred_element_type=F32)
    c = c_ref[...]
    sa = sa_ref[...]
    sb = sb_ref[...]
    q_scale = (DA_HEAD_DIM ** -0.5) * LOG2E
    for h in range(2 * DA_HEADS):
        blk = qk[:, h * LANES:(h + 1) * LANES]
        rot = blk * c + pltpu.roll(blk, 8, axis=1) * sa + pltpu.roll(blk, LANES - 8, axis=1) * sb
        if h < DA_HEADS:
            q_ref[:, h * LANES:(h + 1) * LANES] = (rot * q_scale).astype(BF16)
        else:
            hh = h - DA_HEADS
            k_ref[:, hh * LANES:(hh + 1) * LANES] = rot.astype(BF16)
    v = jnp.dot(x, w_ref[:, POOL_WIDTH + 2 * DA_WIDTH:POOL_WIDTH + 3 * DA_WIDTH], preferred_element_type=F32)
    ones = jnp.ones((V_ROWS - DA_V_DIM, tm), BF16)
    for h in range(DA_HEADS):
        vt_ref[h, 0:DA_V_DIM, :] = v[:, h * LANES:(h + 1) * LANES].T.astype(BF16)
        vt_ref[h, DA_V_DIM:V_ROWS, :] = ones


def _project(x, w_a, rope_c, rope_sa, rope_sb, tm):
    B, S, D = x.shape
    n_a = w_a.shape[1]
    return pl.pallas_call(
        _proj_kernel,
        grid=(B, S // tm),
        in_specs=[pl.BlockSpec((None, tm, D), lambda b_, i: (b_, i, 0)),
                  _const_spec((D, n_a)),
                  pl.BlockSpec((tm, LANES), lambda b_, i: (i, 0)),
                  pl.BlockSpec((tm, LANES), lambda b_, i: (i, 0)),
                  pl.BlockSpec((tm, LANES), lambda b_, i: (i, 0))],
        out_specs=[pl.BlockSpec((None, tm, POOL_WIDTH), lambda b_, i: (b_, i, 0)),
                   pl.BlockSpec((None, tm, DA_WIDTH), lambda b_, i: (b_, i, 0)),
                   pl.BlockSpec((None, tm, DA_WIDTH), lambda b_, i: (b_, i, 0)),
                   pl.BlockSpec((None, DA_HEADS, V_ROWS, tm), lambda b_, i: (b_, 0, 0, i))],
        out_shape=[jax.ShapeDtypeStruct((B, S, POOL_WIDTH), F32),
                   jax.ShapeDtypeStruct((B, S, DA_WIDTH), BF16),
                   jax.ShapeDtypeStruct((B, S, DA_WIDTH), BF16),
                   jax.ShapeDtypeStruct((B, DA_HEADS, V_ROWS, S), BF16)],
        compiler_params=_cparams(("parallel", "parallel")),
        name="proj_qkv",
    )(x, w_a, rope_c, rope_sa, rope_sb)


def _attn_kernel(lam_ref, q_ref, k_ref, vt_ref, g_ref, o_ref, wq_ref, wqn_ref, m_ref, acc_ref, s_ref, tmax_ref,
                 *, tq, tk, qb, unroll):
    S = k_ref.shape[0]
    n_q = S // tq
    n_g = tq // qb
    n_t = S // tk
    lam = lam_ref[0]
    lane = lax.broadcasted_iota(jnp.int32, (LANES, qb), 0)

    def build_query_weights(block, dst_ref):
        for g in range(n_g):
            r0 = pl.multiple_of(block * tq + g * qb, qb)
            qt = q_ref[pl.ds(r0, qb), :].astype(F32).T
            dst_ref[g] = jnp.concatenate([jnp.where(lane < DA_HEAD_DIM, qt, 0.0),
                                          jnp.where(lane >= DA_HEAD_DIM, qt, 0.0)], axis=1).astype(BF16)

    def scores(tile, buf, w_ref=wq_ref):
        c0 = pl.multiple_of(tile * tk, tk)
        kt = k_ref[pl.ds(c0, tk), :]
        for g in range(n_g):
            s = jnp.dot(kt, w_ref[g], preferred_element_type=F32)
            s_ref[buf, g] = s
            tmax_ref[buf, g] = jnp.broadcast_to(jnp.max(s, axis=0, keepdims=True), (SUBLANES, 2 * qb))

    def softmax_pv(tile, buf):
        c0 = pl.multiple_of(tile * tk, tk)
        vt = vt_ref[:, pl.ds(c0, tk)]
        for g in range(n_g):
            m_old = m_ref[g]
            m_new = jnp.maximum(m_old, tmax_ref[buf, g])
            m_ref[g] = m_new
            p = jnp.exp2(s_ref[buf, g] - m_new[0:1, :]).astype(BF16)
            pv = jnp.dot(vt, p, preferred_element_type=F32)
            acc_ref[g] = acc_ref[g] * jnp.exp2(m_old - m_new)[0:1, :] + pv

    def step(tile, parity):
        scores(tile + 1, 1 - parity)
        softmax_pv(tile, parity)

    assert unroll % 2 == 0 and n_t % 2 == 0
    n_loop = (n_t - 1) // unroll

    def kv_body(jj, carry):
        for u in range(unroll):
            step(jj * unroll + u, u % 2)
        return carry

    build_query_weights(0, wqn_ref)
    scores(0, 0, wqn_ref)

    def query_block(qi, carry):
        wq_ref[...] = wqn_ref[...]
        for g in range(n_g):
            m_ref[g] = jnp.full((SUBLANES, 2 * qb), -1e30, F32)
            acc_ref[g] = jnp.zeros((V_ROWS, 2 * qb), F32)
        lax.fori_loop(0, n_loop, kv_body, 0)
        for t in range(n_loop * unroll, n_t - 1):
            step(t, t % 2)
        build_query_weights(jnp.minimum(qi + 1, n_q - 1), wqn_ref)
        scores(0, 0, wqn_ref)
        softmax_pv(n_t - 1, 1)
        for g in range(n_g):
            acc = acc_ref[g]
            o0 = acc[0:DA_V_DIM, 0:qb] / acc[DA_V_DIM:DA_V_DIM + 1, 0:qb]
            o1 = acc[0:DA_V_DIM, qb:2 * qb] / acc[DA_V_DIM:DA_V_DIM + 1, qb:2 * qb]
            o = o0 - lam * o1
            o = o * lax.rsqrt(jnp.mean(o * o, axis=0, keepdims=True) + SUBLN_EPS)
            r0 = pl.multiple_of(qi * tq + g * qb, qb)
            o_ref[pl.ds(r0, qb), :] = (o * g_ref[...]).T
        return carry

    lax.fori_loop(0, n_q, query_block, 0)


def _diff_attention(q, k, vt, lam_arr, g_mat, tq, tk, unroll):
    B, S, _ = q.shape
    n_g = tq // LANES
    kern = functools.partial(_attn_kernel, tq=tq, tk=tk, qb=LANES, unroll=unroll)
    head_block = pl.BlockSpec((None, S, LANES), lambda b_, h: (b_, 0, h))
    return pl.pallas_call(
        kern,
        grid=(B, DA_HEADS),
        in_specs=[pl.BlockSpec(memory_space=pltpu.SMEM),
                  head_block,
                  head_block,
                  pl.BlockSpec((None, None, V_ROWS, S), lambda b_, h: (b_, h, 0, 0)),
                  pl.BlockSpec((DA_V_DIM, LANES), lambda b_, h: (0, 0))],
        out_specs=head_block,
        out_shape=jax.ShapeDtypeStruct((B, S, DA_WIDTH), F32),
        scratch_shapes=[pltpu.VMEM((n_g, LANES, 2 * LANES), BF16),
                        pltpu.VMEM((n_g, LANES, 2 * LANES), BF16),
                        pltpu.VMEM((n_g, SUBLANES, 2 * LANES), F32),
                        pltpu.VMEM((n_g, V_ROWS, 2 * LANES), F32),
                        pltpu.VMEM((2, n_g, tk, 2 * LANES), F32),
                        pltpu.VMEM((2, n_g, SUBLANES, 2 * LANES), F32)],
        compiler_params=_cparams(("parallel", "parallel")),
        name="diff_attn",
    )(lam_arr, q, k, vt, g_mat)


def _memkv_kernel(m_ref, w_ref, km_ref, vm_ref):
    kv = jnp.dot(m_ref[...].astype(BF16), w_ref[...], preferred_element_type=F32)
    km_ref[...] = kv[:, 0:XA_WIDTH].astype(BF16)
    vm_ref[...] = kv[:, XA_WIDTH:2 * XA_WIDTH].astype(BF16)


def _mem_kv(mem, w_mem_kv):
    B, M, D = mem.shape
    return pl.pallas_call(
        _memkv_kernel,
        grid=(B,),
        in_specs=[pl.BlockSpec((None, M, D), lambda b_: (b_, 0, 0)),
                  _const_spec((D, 2 * XA_WIDTH))],
        out_specs=[pl.BlockSpec((None, M, XA_WIDTH), lambda b_: (b_, 0, 0)),
                   pl.BlockSpec((None, M, XA_WIDTH), lambda b_: (b_, 0, 0))],
        out_shape=[jax.ShapeDtypeStruct((B, M, XA_WIDTH), BF16),
                   jax.ShapeDtypeStruct((B, M, XA_WIDTH), BF16)],
        compiler_params=_cparams(("parallel",)),
        name="mem_kv",
    )(mem, w_mem_kv)


def _silu(z):
    return z * (1.0 / (1.0 + jnp.exp(-z)))


def _sigmoid(z):
    return 1.0 / (1.0 + jnp.exp(-z))


def _merge_kernel(x_ref, pu_ref, pup_ref, pun_ref, o_ref, km_ref, vm_ref, wg_ref, pw_ref, ps_ref,
                  wa_ref, wb_ref, wc_ref, wo_ref, g_ref, b_ref, y_ref, ue_ref, *, seq_len, alpha):
    i = pl.program_id(1)
    n_i = pl.num_programs(1)
    tm = x_ref.shape[0]
    x = x_ref[...]
    xb = x.astype(BF16)

    def gate(c0, c1):
        return jnp.dot(xb, wg_ref[:, c0:c1], preferred_element_type=F32)

    xq = gate(1536, 2048).astype(BF16)
    sc = []
    for h in range(XA_HEADS):
        cols = slice(h * XA_HEAD_DIM, (h + 1) * XA_HEAD_DIM)
        sc.append(lax.dot_general(xq[:, cols], km_ref[:, cols], (((1,), (1,)), ((), ())),
                                  preferred_element_type=F32) * (XA_HEAD_DIM ** -0.5))

    yb = o_ref[...] * _silu(gate(512, 1536))
    yb = jnp.dot(yb.astype(BF16), wb_ref[...], preferred_element_type=F32)
    z_pool = gate(0, 512)
    z_mem = gate(2048, 2560)

    yc_parts = []
    for h in range(XA_HEADS):
        cols = slice(h * XA_HEAD_DIM, (h + 1) * XA_HEAD_DIM)
        e = jnp.exp(sc[h] - jnp.max(sc[h], axis=-1, keepdims=True))
        pm = (e / jnp.sum(e, axis=-1, keepdims=True)).astype(BF16)
        yc_parts.append(jnp.dot(pm, vm_ref[:, cols], preferred_element_type=F32))
    yc = jnp.concatenate(yc_parts, axis=1) * _silu(z_mem)
    g_a = _sigmoid(gate(2560, 3584))
    yc = jnp.dot(yc.astype(BF16), wc_ref[...], preferred_element_type=F32)

    u = pu_ref[...]
    ue_ref[0:POOL_HALO, :] = jnp.where(i > 0, pup_ref[...], 0.0)
    ue_ref[POOL_HALO:POOL_HALO + tm, :] = u
    ue_ref[POOL_HALO + tm:2 * POOL_HALO + tm, :] = jnp.where(i < n_i - 1, pun_ref[...], 0.0)
    t = (lax.broadcasted_iota(jnp.int32, (tm, POOL_GROUP_WIDTH), 0) + i * tm).astype(F32)
    ya_parts = []
    for g, w in enumerate(POOL_WINDOWS):
        cols = slice(g * POOL_GROUP_WIDTH, (g + 1) * POOL_GROUP_WIDTH)
        part, span = ue_ref[:, cols], 1
        while span < w:
            n = part.shape[0] - span
            part, span = part[0:n] + part[span:span + n], 2 * span
        r0 = POOL_HALO - w // 2
        wsum = part[r0:r0 + tm]
        cnt = w - jnp.maximum(w // 2 - t, 0.0) - jnp.maximum(t - (seq_len - w // 2), 0.0)
        d = (wsum / cnt - u[:, cols]).astype(BF16)
        ya_parts.append(jnp.dot(d, pw_ref[g], preferred_element_type=F32))
    ya = jnp.concatenate(ya_parts, axis=1) * ps_ref[...] * _silu(z_pool)
    g_b = _sigmoid(gate(3584, 4608))
    ya = jnp.dot(ya.astype(BF16), wa_ref[...], preferred_element_type=F32)

    merged = g_a * ya + g_b * yb + _sigmoid(gate(4608, 5632)) * yc
    out = jnp.dot(merged.astype(BF16), wo_ref[...], preferred_element_type=F32)
    y_ref[...] = _ln_rows(alpha * x + out, g_ref[...], b_ref[...])


def _merge(x, pu, o, km, vm, w_g, pool_w, pool_scale, w_a, w_b, w_c, w_o, ln_g, ln_b, tm, alpha):
    B, S, D = x.shape
    hb = tm // POOL_HALO
    n_hb = S // POOL_HALO
    kern = functools.partial(_merge_kernel, seq_len=S, alpha=alpha)
    tile = lambda w: pl.BlockSpec((None, tm, w), lambda b_, i: (b_, i, 0))
    return pl.pallas_call(
        kern,
        grid=(B, S // tm),
        in_specs=[tile(D),
                  tile(POOL_WIDTH),
                  pl.BlockSpec((None, POOL_HALO, POOL_WIDTH), lambda b_, i: (b_, jnp.maximum(i * hb - 1, 0), 0)),
                  pl.BlockSpec((None, POOL_HALO, POOL_WIDTH),
                               lambda b_, i: (b_, jnp.minimum((i + 1) * hb, n_hb - 1), 0)),
                  tile(DA_WIDTH),
                  pl.BlockSpec((None, N_MEM, XA_WIDTH), lambda b_, i: (b_, 0, 0)),
                  pl.BlockSpec((None, N_MEM, XA_WIDTH), lambda b_, i: (b_, 0, 0)),
                  _const_spec(w_g.shape),
                  _const_spec(pool_w.shape),
                  _const_spec((1, POOL_WIDTH)),
                  _const_spec(w_a.shape), _const_spec(w_b.shape), _const_spec(w_c.shape), _const_spec(w_o.shape),
                  _const_spec((1, D)), _const_spec((1, D))],
        out_specs=tile(D),
        out_shape=jax.ShapeDtypeStruct((B, S, D), F32),
        scratch_shapes=[pltpu.VMEM((tm + 2 * POOL_HALO, POOL_WIDTH), F32)],
        compiler_params=_cparams(("parallel", "arbitrary")),
        name="merge",
    )(x, pu, pu, pu, o, km, vm, w_g, pool_w, pool_scale.reshape(1, POOL_WIDTH), w_a, w_b, w_c, w_o,
      ln_g.reshape(1, D), ln_b.reshape(1, D))


def _rope_tables(seq):
    pos = jnp.arange(seq, dtype=F32)
    inv = ROPE_THETA ** (-jnp.arange(0, ROT_DIM, 2, dtype=F32) / ROT_DIM)
    ang = pos[:, None] * inv[None, :]
    cos, sin = jnp.cos(ang), jnp.sin(ang)
    half = ROT_DIM // 2
    lane = jnp.arange(LANES) % DA_HEAD_DIM
    idx = lane % half
    c = jnp.where(lane[None, :] < ROT_DIM, cos[:, idx], 1.0)
    sa = jnp.where((lane[None, :] >= half) & (lane[None, :] < ROT_DIM), sin[:, idx], 0.0)
    sb = jnp.where(lane[None, :] < half, -sin[:, idx], 0.0)
    return c.astype(F32), sa.astype(F32), sb.astype(F32)


def _tiles(S):
    if S >= 8192:
        attn = dict(tq=512, tk=512, unroll=4)
    else:
        attn = dict(tq=1024, tk=256, unroll=4)
    return dict(tm_ln=512, tm_proj=512, tm_merge=512, **attn)


def _trunk(x, mem, ln_in_g, ln_in_b, layers):
    B, S, _ = x.shape
    depth = len(layers)
    alpha = (2.0 * depth) ** 0.25
    t = _tiles(S)
    rope_c, rope_sa, rope_sb = _rope_tables(S)
    x = _layernorm(x, ln_in_g, ln_in_b, t["tm_ln"])
    for L in layers:
        pu, q, k, vt = _project(x, L["w_a"], rope_c, rope_sa, rope_sb, t["tm_proj"])
        o = _diff_attention(q, k, vt, L["lam"], L["subln"], t["tq"], t["tk"], t["unroll"])
        km, vm = _mem_kv(mem, L["w_mem_kv"])
        x = _merge(x, pu, o, km, vm, L["w_g"], L["pool_w"], L["pool_scale"], L["w_br_a"], L["w_br_b"],
                   L["w_br_c"], L["w_out"], L["ln_g"], L["ln_b"], t["tm_merge"], alpha)
    return x


def _prepare_layers(w_in, w_mem_kv, pool_w, pool_scale, lam_q1, lam_k1, lam_q2, lam_k2, subln_g,
                    w_br_a, w_br_b, w_br_c, w_out, ln_g, ln_b):
    layers = []
    for i in range(w_in.shape[0]):
        lam_init = 0.8 - 0.6 * math.exp(-0.3 * i)
        lam = (jnp.exp(jnp.sum(lam_q1[i] * lam_k1[i])) - jnp.exp(jnp.sum(lam_q2[i] * lam_k2[i])) + lam_init)
        wi = w_in[i].astype(BF16)
        layers.append(dict(
            w_a=jnp.concatenate([wi[:, C_PU:C_PZ], wi[:, C_Q:C_AZ]], axis=1),
            w_g=jnp.concatenate([wi[:, C_PZ:C_Q], wi[:, C_AZ:C_END]], axis=1),
            w_mem_kv=w_mem_kv[i].astype(BF16),
            pool_w=pool_w[i].astype(BF16),
            pool_scale=pool_scale[i],
            lam=jnp.reshape(lam, (1,)).astype(F32),
            subln=jnp.broadcast_to((subln_g[i] * (1.0 - lam_init))[:, None], (DA_V_DIM, LANES)).astype(F32),
            w_br_a=w_br_a[i].astype(BF16), w_br_b=w_br_b[i].astype(BF16), w_br_c=w_br_c[i].astype(BF16),
            w_out=w_out[i].astype(BF16), ln_g=ln_g[i], ln_b=ln_b[i]))
    return layers


def kernel(x_prompt, x_sample, mem_prompt, mem_sample, ln_in_g, ln_in_b, w_in, w_mem_kv, pool_w, pool_scale,
           lam_q1, lam_k1, lam_q2, lam_k2, subln_g, w_br_a, w_br_b, w_br_c, w_out, ln_g, ln_b):
    layers = _prepare_layers(w_in, w_mem_kv, pool_w, pool_scale, lam_q1, lam_k1, lam_q2, lam_k2, subln_g,
                             w_br_a, w_br_b, w_br_c, w_out, ln_g, ln_b)
    y_prompt = _trunk(x_prompt, mem_prompt, ln_in_g, ln_in_b, layers)
    y_sample = _trunk(x_sample, mem_sample, ln_in_g, ln_in_b, layers)
    return (y_prompt, y_sample)
```

```python
import functools
import math

import jax
import jax.numpy as jnp
from jax import lax
from jax.experimental import pallas as pl
from jax.experimental.pallas import tpu as pltpu

F32 = jnp.float32
BF16 = jnp.bfloat16

D_MODEL = 1024
POOL_GROUPS = 4
POOL_GROUP_WIDTH = 128
POOL_WIDTH = POOL_GROUPS * POOL_GROUP_WIDTH
POOL_WINDOWS = (2, 4, 8, 16)
POOL_HALO = 8
DA_HEADS = 8
DA_HEAD_DIM = 64
DA_V_DIM = 2 * DA_HEAD_DIM
DA_WIDTH = DA_HEADS * DA_V_DIM
ROT_DIM = DA_HEAD_DIM // 4
ROPE_THETA = 500000.0
SUBLN_EPS = 1e-5
N_MEM = 256
XA_HEADS = 4
XA_HEAD_DIM = 128
XA_WIDTH = XA_HEADS * XA_HEAD_DIM
LN_EPS = 1e-5
LOG2E = 1.4426950408889634

C_PU, C_PZ, C_Q, C_K, C_V, C_AZ, C_XQ, C_XZ, C_GL, C_END = 0, 512, 1024, 2048, 3072, 4096, 5120, 5632, 6144, 9216

V_ROWS = DA_V_DIM + 16
LANES = 128
SUBLANES = 8
VMEM_LIMIT = 56 * 1024 * 1024


def _cparams(sem):
    return pltpu.CompilerParams(dimension_semantics=sem, vmem_limit_bytes=VMEM_LIMIT)


def _const_spec(shape):
    n = len(shape)
    return pl.BlockSpec(shape, lambda *_: (0,) * n, pipeline_mode=pl.Buffered(1))


def _ln_rows(x, g, b):
    mu = jnp.mean(x, axis=-1, keepdims=True)
    xc = x - mu
    var = jnp.mean(xc * xc, axis=-1, keepdims=True)
    return xc * lax.rsqrt(var + LN_EPS) * g + b


def _ln_kernel(x_ref, g_ref, b_ref, y_ref):
    y_ref[...] = _ln_rows(x_ref[...], g_ref[...], b_ref[...])


def _layernorm(x, g, b, tm):
    B, S, D = x.shape
    return pl.pallas_call(
        _ln_kernel,
        grid=(B, S // tm),
        in_specs=[pl.BlockSpec((None, tm, D), lambda b_, i: (b_, i, 0)),
                  _const_spec((1, D)), _const_spec((1, D))],
        out_specs=pl.BlockSpec((None, tm, D), lambda b_, i: (b_, i, 0)),
        out_shape=jax.ShapeDtypeStruct((B, S, D), F32),
        compiler_params=_cparams(("parallel", "parallel")),
        name="ln_in",
    )(x, g.reshape(1, D), b.reshape(1, D))


def _proj_kernel(x_ref, w_ref, c_ref, sa_ref, sb_ref, pu_ref, q_ref, k_ref, vt_ref):
    x = x_ref[...].astype(BF16)
    tm = x.shape[0]
    pu_ref[...] = jnp.dot(x, w_ref[:, 0:POOL_WIDTH], preferred_element_type=F32)
    qk = jnp.dot(x, w_ref[:, POOL_WIDTH:POOL_WIDTH + 2 * DA_WIDTH], preferred_element_type=F32)
    c = c_ref[...]
    sa = sa_ref[...]
    sb = sb_ref[...]
    q_scale = (DA_HEAD_DIM ** -0.5) * LOG2E
    for h in range(2 * DA_HEADS):
        blk = qk[:, h * LANES:(h + 1) * LANES]
        rot = blk * c + pltpu.roll(blk, 8, axis=1) * sa + pltpu.roll(blk, LANES - 8, axis=1) * sb
        if h < DA_HEADS:
            q_ref[:, h * LANES:(h + 1) * LANES] = (rot * q_scale).astype(BF16)
        else:
            hh = h - DA_HEADS
            k_ref[:, hh * LANES:(hh + 1) * LANES] = rot.astype(BF16)
    v = jnp.dot(x, w_ref[:, POOL_WIDTH + 2 * DA_WIDTH:POOL_WIDTH + 3 * DA_WIDTH], preferred_element_type=F32)
    ones = jnp.ones((V_ROWS - DA_V_DIM, tm), BF16)
    for h in range(DA_HEADS):
        vt_ref[h, 0:DA_V_DIM, :] = v[:, h * LANES:(h + 1) * LANES].T.astype(BF16)
        vt_ref[h, DA_V_DIM:V_ROWS, :] = ones


def _project(x, w_a, rope_c, rope_sa, rope_sb, tm):
    B, S, D = x.shape
    n_a = w_a.shape[1]
    return pl.pallas_call(
        _proj_kernel,
        grid=(B, S // tm),
        in_specs=[pl.BlockSpec((None, tm, D), lambda b_, i: (b_, i, 0)),
                  _const_spec((D, n_a)),
                  pl.BlockSpec((tm, LANES), lambda b_, i: (i, 0)),
                  pl.BlockSpec((tm, LANES), lambda b_, i: (i, 0)),
                  pl.BlockSpec((tm, LANES), lambda b_, i: (i, 0))],
        out_specs=[pl.BlockSpec((None, tm, POOL_WIDTH), lambda b_, i: (b_, i, 0)),
                   pl.BlockSpec((None, tm, DA_WIDTH), lambda b_, i: (b_, i, 0)),
                   pl.BlockSpec((None, tm, DA_WIDTH), lambda b_, i: (b_, i, 0)),
                   pl.BlockSpec((None, DA_HEADS, V_ROWS, tm), lambda b_, i: (b_, 0, 0, i))],
        out_shape=[jax.ShapeDtypeStruct((B, S, POOL_WIDTH), F32),
                   jax.ShapeDtypeStruct((B, S, DA_WIDTH), BF16),
                   jax.ShapeDtypeStruct((B, S, DA_WIDTH), BF16),
                   jax.ShapeDtypeStruct((B, DA_HEADS, V_ROWS, S), BF16)],
        compiler_params=_cparams(("parallel", "parallel")),
        name="proj_qkv",
    )(x, w_a, rope_c, rope_sa, rope_sb)


def _attn_kernel(lam_ref, q_ref, k_ref, vt_ref, g_ref, o_ref, wq_ref, wqn_ref, m_ref, acc_ref, s_ref, tmax_ref,
                 *, tq, tk, qb, unroll):
    S = k_ref.shape[0]
    n_q = S // tq
    n_g = tq // qb
    n_t = S // tk
    lam = lam_ref[0]
    lane = lax.broadcasted_iota(jnp.int32, (LANES, qb), 0)

    def build_query_weights(block, dst_ref):
        for g in range(n_g):
            r0 = pl.multiple_of(block * tq + g * qb, qb)
            qt = q_ref[pl.ds(r0, qb), :].astype(F32).T
            dst_ref[g] = jnp.concatenate([jnp.where(lane < DA_HEAD_DIM, qt, 0.0),
                                          jnp.where(lane >= DA_HEAD_DIM, qt, 0.0)], axis=1).astype(BF16)

    all_groups = tuple(range(n_g))

    def scores(tile, buf, w_ref=wq_ref, groups=all_groups):
        c0 = pl.multiple_of(tile * tk, tk)
        kt = k_ref[pl.ds(c0, tk), :]
        for g in groups:
            s = jnp.dot(kt, w_ref[g], preferred_element_type=F32)
            s_ref[buf, g] = s
            tmax_ref[buf, g] = jnp.broadcast_to(jnp.max(s, axis=0, keepdims=True), (SUBLANES, 2 * qb))

    def softmax_pv(tile, buf, groups=all_groups):
        c0 = pl.multiple_of(tile * tk, tk)
        vt = vt_ref[:, pl.ds(c0, tk)]
        for g in groups:
            m_old = m_ref[g]
            m_new = jnp.maximum(m_old, tmax_ref[buf, g])
            m_ref[g] = m_new
            p = jnp.exp2(s_ref[buf, g] - m_new[0:1, :]).astype(BF16)
            pv = jnp.dot(vt, p, preferred_element_type=F32)
            acc_ref[g] = acc_ref[g] * jnp.exp2(m_old - m_new)[0:1, :] + pv

    def step(tile, parity):
        for g in all_groups:
            scores(tile + 1, 1 - parity, groups=(g,))
            softmax_pv(tile, parity, groups=(g,))

    assert unroll % 2 == 0 and n_t % 2 == 0
    n_loop = (n_t - 1) // unroll

    def kv_body(jj, carry):
        for u in range(unroll):
            step(jj * unroll + u, u % 2)
        return carry

    build_query_weights(0, wqn_ref)
    scores(0, 0, wqn_ref)

    def query_block(qi, carry):
        wq_ref[...] = wqn_ref[...]
        for g in range(n_g):
            m_ref[g] = jnp.full((SUBLANES, 2 * qb), -1e30, F32)
            acc_ref[g] = jnp.zeros((V_ROWS, 2 * qb), F32)
        lax.fori_loop(0, n_loop, kv_body, 0)
        for t in range(n_loop * unroll, n_t - 1):
            step(t, t % 2)
        build_query_weights(jnp.minimum(qi + 1, n_q - 1), wqn_ref)
        scores(0, 0, wqn_ref)
        softmax_pv(n_t - 1, 1)
        for g in range(n_g):
            acc = acc_ref[g]
            o0 = acc[0:DA_V_DIM, 0:qb] / acc[DA_V_DIM:DA_V_DIM + 1, 0:qb]
            o1 = acc[0:DA_V_DIM, qb:2 * qb] / acc[DA_V_DIM:DA_V_DIM + 1, qb:2 * qb]
            o = o0 - lam * o1
            o = o * lax.rsqrt(jnp.mean(o * o, axis=0, keepdims=True) + SUBLN_EPS)
            r0 = pl.multiple_of(qi * tq + g * qb, qb)
            o_ref[pl.ds(r0, qb), :] = (o * g_ref[...]).T
        return carry

    lax.fori_loop(0, n_q, query_block, 0)


def _diff_attention(q, k, vt, lam_arr, g_mat, tq, tk, unroll):
    B, S, _ = q.shape
    n_g = tq // LANES
    kern = functools.partial(_attn_kernel, tq=tq, tk=tk, qb=LANES, unroll=unroll)
    head_block = pl.BlockSpec((None, S, LANES), lambda b_, h: (b_, 0, h))
    return pl.pallas_call(
        kern,
        grid=(B, DA_HEADS),
        in_specs=[pl.BlockSpec(memory_space=pltpu.SMEM),
                  head_block,
                  head_block,
                  pl.BlockSpec((None, None, V_ROWS, S), lambda b_, h: (b_, h, 0, 0)),
                  pl.BlockSpec((DA_V_DIM, LANES), lambda b_, h: (0, 0))],
        out_specs=head_block,
        out_shape=jax.ShapeDtypeStruct((B, S, DA_WIDTH), F32),
        scratch_shapes=[pltpu.VMEM((n_g, LANES, 2 * LANES), BF16),
                        pltpu.VMEM((n_g, LANES, 2 * LANES), BF16),
                        pltpu.VMEM((n_g, SUBLANES, 2 * LANES), F32),
                        pltpu.VMEM((n_g, V_ROWS, 2 * LANES), F32),
                        pltpu.VMEM((2, n_g, tk, 2 * LANES), F32),
                        pltpu.VMEM((2, n_g, SUBLANES, 2 * LANES), F32)],
        compiler_params=_cparams(("parallel", "parallel")),
        name="diff_attn",
    )(lam_arr, q, k, vt, g_mat)


def _memkv_kernel(m_ref, w_ref, km_ref, vm_ref):
    kv = jnp.dot(m_ref[...].astype(BF16), w_ref[...], preferred_element_type=F32)
    km_ref[...] = kv[:, 0:XA_WIDTH].astype(BF16)
    vm_ref[...] = kv[:, XA_WIDTH:2 * XA_WIDTH].astype(BF16)


def _mem_kv(mem, w_mem_kv):
    B, M, D = mem.shape
    return pl.pallas_call(
        _memkv_kernel,
        grid=(B,),
        in_specs=[pl.BlockSpec((None, M, D), lambda b_: (b_, 0, 0)),
                  _const_spec((D, 2 * XA_WIDTH))],
        out_specs=[pl.BlockSpec((None, M, XA_WIDTH), lambda b_: (b_, 0, 0)),
                   pl.BlockSpec((None, M, XA_WIDTH), lambda b_: (b_, 0, 0))],
        out_shape=[jax.ShapeDtypeStruct((B, M, XA_WIDTH), BF16),
                   jax.ShapeDtypeStruct((B, M, XA_WIDTH), BF16)],
        compiler_params=_cparams(("parallel",)),
        name="mem_kv",
    )(mem, w_mem_kv)


def _silu(z):
    return z * (1.0 / (1.0 + jnp.exp(-z)))


def _sigmoid(z):
    return 1.0 / (1.0 + jnp.exp(-z))


def _merge_kernel(x_ref, pu_ref, pup_ref, pun_ref, o_ref, km_ref, vm_ref, wg_ref, pw_ref, ps_ref,
                  wa_ref, wb_ref, wc_ref, wo_ref, g_ref, b_ref, y_ref, ue_ref, *, seq_len, alpha):
    i = pl.program_id(1)
    n_i = pl.num_programs(1)
    tm = x_ref.shape[0]
    x = x_ref[...]
    xb = x.astype(BF16)

    def gate(c0, c1):
        return jnp.dot(xb, wg_ref[:, c0:c1], preferred_element_type=F32)

    xq = gate(1536, 2048).astype(BF16)
    sc = []
    for h in range(XA_HEADS):
        cols = slice(h * XA_HEAD_DIM, (h + 1) * XA_HEAD_DIM)
        sc.append(lax.dot_general(xq[:, cols], km_ref[:, cols], (((1,), (1,)), ((), ())),
                                  preferred_element_type=F32) * (XA_HEAD_DIM ** -0.5))

    yb = o_ref[...] * _silu(gate(512, 1536))
    yb = jnp.dot(yb.astype(BF16), wb_ref[...], preferred_element_type=F32)
    z_pool = gate(0, 512)
    z_mem = gate(2048, 2560)

    yc_parts = []
    for h in range(XA_HEADS):
        cols = slice(h * XA_HEAD_DIM, (h + 1) * XA_HEAD_DIM)
        e = jnp.exp(sc[h] - jnp.max(sc[h], axis=-1, keepdims=True))
        pm = (e / jnp.sum(e, axis=-1, keepdims=True)).astype(BF16)
        yc_parts.append(jnp.dot(pm, vm_ref[:, cols], preferred_element_type=F32))
    yc = jnp.concatenate(yc_parts, axis=1) * _silu(z_mem)
    g_a = _sigmoid(gate(2560, 3584))
    yc = jnp.dot(yc.astype(BF16), wc_ref[...], preferred_element_type=F32)

    u = pu_ref[...]
    ue_ref[0:POOL_HALO, :] = jnp.where(i > 0, pup_ref[...], 0.0)
    ue_ref[POOL_HALO:POOL_HALO + tm, :] = u
    ue_ref[POOL_HALO + tm:2 * POOL_HALO + tm, :] = jnp.where(i < n_i - 1, pun_ref[...], 0.0)
    t = (lax.broadcasted_iota(jnp.int32, (tm, POOL_GROUP_WIDTH), 0) + i * tm).astype(F32)
    ya_parts = []
    for g, w in enumerate(POOL_WINDOWS):
        cols = slice(g * POOL_GROUP_WIDTH, (g + 1) * POOL_GROUP_WIDTH)
        part, span = ue_ref[:, cols], 1
        while span < w:
            n = part.shape[0] - span
            part, span = part[0:n] + part[span:span + n], 2 * span
        r0 = POOL_HALO - w // 2
        wsum = part[r0:r0 + tm]
        cnt = w - jnp.maximum(w // 2 - t, 0.0) - jnp.maximum(t - (seq_len - w // 2), 0.0)
        d = (wsum / cnt - u[:, cols]).astype(BF16)
        ya_parts.append(jnp.dot(d, pw_ref[g], preferred_element_type=F32))
    ya = jnp.concatenate(ya_parts, axis=1) * ps_ref[...] * _silu(z_pool)
    g_b = _sigmoid(gate(3584, 4608))
    ya = jnp.dot(ya.astype(BF16), wa_ref[...], preferred_element_type=F32)

    merged = g_a * ya + g_b * yb + _sigmoid(gate(4608, 5632)) * yc
    out = jnp.dot(merged.astype(BF16), wo_ref[...], preferred_element_type=F32)
    y_ref[...] = _ln_rows(alpha * x + out, g_ref[...], b_ref[...])


def _merge(x, pu, o, km, vm, w_g, pool_w, pool_scale, w_a, w_b, w_c, w_o, ln_g, ln_b, tm, alpha):
    B, S, D = x.shape
    hb = tm // POOL_HALO
    n_hb = S // POOL_HALO
    kern = functools.partial(_merge_kernel, seq_len=S, alpha=alpha)
    tile = lambda w: pl.BlockSpec((None, tm, w), lambda b_, i: (b_, i, 0))
    return pl.pallas_call(
        kern,
        grid=(B, S // tm),
        in_specs=[tile(D),
                  tile(POOL_WIDTH),
                  pl.BlockSpec((None, POOL_HALO, POOL_WIDTH), lambda b_, i: (b_, jnp.maximum(i * hb - 1, 0), 0)),
                  pl.BlockSpec((None, POOL_HALO, POOL_WIDTH),
                               lambda b_, i: (b_, jnp.minimum((i + 1) * hb, n_hb - 1), 0)),
                  tile(DA_WIDTH),
                  pl.BlockSpec((None, N_MEM, XA_WIDTH), lambda b_, i: (b_, 0, 0)),
                  pl.BlockSpec((None, N_MEM, XA_WIDTH), lambda b_, i: (b_, 0, 0)),
                  _const_spec(w_g.shape),
                  _const_spec(pool_w.shape),
                  _const_spec((1, POOL_WIDTH)),
                  _const_spec(w_a.shape), _const_spec(w_b.shape), _const_spec(w_c.shape), _const_spec(w_o.shape),
                  _const_spec((1, D)), _const_spec((1, D))],
        out_specs=tile(D),
        out_shape=jax.ShapeDtypeStruct((B, S, D), F32),
        scratch_shapes=[pltpu.VMEM((tm + 2 * POOL_HALO, POOL_WIDTH), F32)],
        compiler_params=_cparams(("parallel", "arbitrary")),
        name="merge",
    )(x, pu, pu, pu, o, km, vm, w_g, pool_w, pool_scale.reshape(1, POOL_WIDTH), w_a, w_b, w_c, w_o,
      ln_g.reshape(1, D), ln_b.reshape(1, D))


def _rope_tables(seq):
    pos = jnp.arange(seq, dtype=F32)
    inv = ROPE_THETA ** (-jnp.arange(0, ROT_DIM, 2, dtype=F32) / ROT_DIM)
    ang = pos[:, None] * inv[None, :]
    cos, sin = jnp.cos(ang), jnp.sin(ang)
    half = ROT_DIM // 2
    lane = jnp.arange(LANES) % DA_HEAD_DIM
    idx = lane % half
    c = jnp.where(lane[None, :] < ROT_DIM, cos[:, idx], 1.0)
    sa = jnp.where((lane[None, :] >= half) & (lane[None, :] < ROT_DIM), sin[:, idx], 0.0)
    sb = jnp.where(lane[None, :] < half, -sin[:, idx], 0.0)
    return c.astype(F32), sa.astype(F32), sb.astype(F32)


def _tiles(S):
    if S >= 8192:
        attn = dict(tq=1024, tk=512, unroll=4)
    else:
        attn = dict(tq=2048, tk=512, unroll=2)
    return dict(tm_ln=512, tm_proj=512, tm_merge=512, **attn)


def _trunk(x, mem, ln_in_g, ln_in_b, layers):
    B, S, _ = x.shape
    depth = len(layers)
    alpha = (2.0 * depth) ** 0.25
    t = _tiles(S)
    rope_c, rope_sa, rope_sb = _rope_tables(S)
    x = _layernorm(x, ln_in_g, ln_in_b, t["tm_ln"])
    for L in layers:
        pu, q, k, vt = _project(x, L["w_a"], rope_c, rope_sa, rope_sb, t["tm_proj"])
        o = _diff_attention(q, k, vt, L["lam"], L["subln"], t["tq"], t["tk"], t["unroll"])
        km, vm = _mem_kv(mem, L["w_mem_kv"])
        x = _merge(x, pu, o, km, vm, L["w_g"], L["pool_w"], L["pool_scale"], L["w_br_a"], L["w_br_b"],
                   L["w_br_c"], L["w_out"], L["ln_g"], L["ln_b"], t["tm_merge"], alpha)
    return x


def _prepare_layers(w_in, w_mem_kv, pool_w, pool_scale, lam_q1, lam_k1, lam_q2, lam_k2, subln_g,
                    w_br_a, w_br_b, w_br_c, w_out, ln_g, ln_b):
    layers = []
    for i in range(w_in.shape[0]):
        lam_init = 0.8 - 0.6 * math.exp(-0.3 * i)
        lam = (jnp.exp(jnp.sum(lam_q1[i] * lam_k1[i])) - jnp.exp(jnp.sum(lam_q2[i] * lam_k2[i])) + lam_init)
        wi = w_in[i].astype(BF16)
        layers.append(dict(
            w_a=jnp.concatenate([wi[:, C_PU:C_PZ], wi[:, C_Q:C_AZ]], axis=1),
            w_g=jnp.concatenate([wi[:, C_PZ:C_Q], wi[:, C_AZ:C_END]], axis=1),
            w_mem_kv=w_mem_kv[i].astype(BF16),
            pool_w=pool_w[i].astype(BF16),
            pool_scale=pool_scale[i],
            lam=jnp.reshape(lam, (1,)).astype(F32),
            subln=jnp.broadcast_to((subln_g[i] * (1.0 - lam_init))[:, None], (DA_V_DIM, LANES)).astype(F32),
            w_br_a=w_br_a[i].astype(BF16), w_br_b=w_br_b[i].astype(BF16), w_br_c=w_br_c[i].astype(BF16),
            w_out=w_out[i].astype(BF16), ln_g=ln_g[i], ln_b=ln_b[i]))
    return layers


def kernel(x_prompt, x_sample, mem_prompt, mem_sample, ln_in_g, ln_in_b, w_in, w_mem_kv, pool_w, pool_scale,
           lam_q1, lam_k1, lam_q2, lam_k2, subln_g, w_br_a, w_br_b, w_br_c, w_out, ln_g, ln_b):
    layers = _prepare_layers(w_in, w_mem_kv, pool_w, pool_scale, lam_q1, lam_k1, lam_q2, lam_k2, subln_g,
                             w_br_a, w_br_b, w_br_c, w_out, ln_g, ln_b)
    y_prompt = _trunk(x_prompt, mem_prompt, ln_in_g, ln_in_b, layers)
    y_sample = _trunk(x_sample, mem_sample, ln_in_g, ln_in_b, layers)
    return (y_prompt, y_sample)
```

```python
import functools
import math

import jax
import jax.numpy as jnp
from jax import lax
from jax.experimental import pallas as pl
from jax.experimental.pallas import tpu as pltpu

F32 = jnp.float32
BF16 = jnp.bfloat16

D_MODEL = 1024
POOL_GROUPS = 4
POOL_GROUP_WIDTH = 128
POOL_WIDTH = POOL_GROUPS * POOL_GROUP_WIDTH
POOL_WINDOWS = (2, 4, 8, 16)
POOL_HALO = 8
DA_HEADS = 8
DA_HEAD_DIM = 64
DA_V_DIM = 2 * DA_HEAD_DIM
DA_WIDTH = DA_HEADS * DA_V_DIM
ROT_DIM = DA_HEAD_DIM // 4
ROPE_THETA = 500000.0
SUBLN_EPS = 1e-5
N_MEM = 256
XA_HEADS = 4
XA_HEAD_DIM = 128
XA_WIDTH = XA_HEADS * XA_HEAD_DIM
LN_EPS = 1e-5
LOG2E = 1.4426950408889634

C_PU, C_PZ, C_Q, C_K, C_V, C_AZ, C_XQ, C_XZ, C_GL, C_END = 0, 512, 1024, 2048, 3072, 4096, 5120, 5632, 6144, 9216

V_ROWS = DA_V_DIM + 16
LANES = 128
SUBLANES = 8
VMEM_LIMIT = 56 * 1024 * 1024


def _cparams(sem):
    return pltpu.CompilerParams(dimension_semantics=sem, vmem_limit_bytes=VMEM_LIMIT)


def _const_spec(shape):
    n = len(shape)
    return pl.BlockSpec(shape, lambda *_: (0,) * n, pipeline_mode=pl.Buffered(1))


def _ln_rows(x, g, b):
    mu = jnp.mean(x, axis=-1, keepdims=True)
    xc = x - mu
    var = jnp.mean(xc * xc, axis=-1, keepdims=True)
    return xc * lax.rsqrt(var + LN_EPS) * g + b


def _ln_kernel(x_ref, g_ref, b_ref, y_ref):
    y_ref[...] = _ln_rows(x_ref[...], g_ref[...], b_ref[...])


def _layernorm(x, g, b, tm):
    B, S, D = x.shape
    return pl.pallas_call(
        _ln_kernel,
        grid=(B, S // tm),
        in_specs=[pl.BlockSpec((None, tm, D), lambda b_, i: (b_, i, 0)),
                  _const_spec((1, D)), _const_spec((1, D))],
        out_specs=pl.BlockSpec((None, tm, D), lambda b_, i: (b_, i, 0)),
        out_shape=jax.ShapeDtypeStruct((B, S, D), F32),
        compiler_params=_cparams(("parallel", "parallel")),
        name="ln_in",
    )(x, g.reshape(1, D), b.reshape(1, D))


def _proj_kernel(x_ref, w_ref, c_ref, sa_ref, sb_ref, pu_ref, q_ref, k_ref, vt_ref):
    x = x_ref[...].astype(BF16)
    tm = x.shape[0]
    pu_ref[...] = jnp.dot(x, w_ref[:, 0:POOL_WIDTH], preferred_element_type=F32)
    qk = jnp.dot(x, w_ref[:, POOL_WIDTH:POOL_WIDTH + 2 * DA_WIDTH], preferred_element_type=F32)
    c = c_ref[...]
    sa = sa_ref[...]
    sb = sb_ref[...]
    q_scale = (DA_HEAD_DIM ** -0.5) * LOG2E
    for h in range(2 * DA_HEADS):
        blk = qk[:, h * LANES:(h + 1) * LANES]
        rot = blk * c + pltpu.roll(blk, 8, axis=1) * sa + pltpu.roll(blk, LANES - 8, axis=1) * sb
        if h < DA_HEADS:
            q_ref[:, h * LANES:(h + 1) * LANES] = (rot * q_scale).astype(BF16)
        else:
            hh = h - DA_HEADS
            k_ref[:, hh * LANES:(hh + 1) * LANES] = rot.astype(BF16)
    v = jnp.dot(x, w_ref[:, POOL_WIDTH + 2 * DA_WIDTH:POOL_WIDTH + 3 * DA_WIDTH], preferred_element_type=F32)
    ones = jnp.ones((V_ROWS - DA_V_DIM, tm), BF16)
    for h in range(DA_HEADS):
        vt_ref[h, 0:DA_V_DIM, :] = v[:, h * LANES:(h + 1) * LANES].T.astype(BF16)
        vt_ref[h, DA_V_DIM:V_ROWS, :] = ones


def _project(x, w_a, rope_c, rope_sa, rope_sb, tm):
    B, S, D = x.shape
    n_a = w_a.shape[1]
    return pl.pallas_call(
        _proj_kernel,
        grid=(B, S // tm),
        in_specs=[pl.BlockSpec((None, tm, D), lambda b_, i: (b_, i, 0)),
                  _const_spec((D, n_a)),
                  pl.BlockSpec((tm, LANES), lambda b_, i: (i, 0)),
                  pl.BlockSpec((tm, LANES), lambda b_, i: (i, 0)),
                  pl.BlockSpec((tm, LANES), lambda b_, i: (i, 0))],
        out_specs=[pl.BlockSpec((None, tm, POOL_WIDTH), lambda b_, i: (b_, i, 0)),
                   pl.BlockSpec((None, tm, DA_WIDTH), lambda b_, i: (b_, i, 0)),
                   pl.BlockSpec((None, tm, DA_WIDTH), lambda b_, i: (b_, i, 0)),
                   pl.BlockSpec((None, DA_HEADS, V_ROWS, tm), lambda b_, i: (b_, 0, 0, i))],
        out_shape=[jax.ShapeDtypeStruct((B, S, POOL_WIDTH), F32),
                   jax.ShapeDtypeStruct((B, S, DA_WIDTH), BF16),
                   jax.ShapeDtypeStruct((B, S, DA_WIDTH), BF16),
                   jax.ShapeDtypeStruct((B, DA_HEADS, V_ROWS, S), BF16)],
        compiler_params=_cparams(("parallel", "parallel")),
        name="proj_qkv",
    )(x, w_a, rope_c, rope_sa, rope_sb)


def _attn_kernel(lam_ref, q_ref, k_ref, vt_ref, g_ref, o_ref, wq_ref, m_ref, acc_ref, s_ref, tmax_ref,
                 *, tq, tk, qb, unroll):
    S = k_ref.shape[0]
    n_q = S // tq
    n_g = tq // qb
    n_t = S // tk
    lam = lam_ref[0]
    lane = lax.broadcasted_iota(jnp.int32, (LANES, qb), 0)

    def build_query_weights(block, g):
        r0 = pl.multiple_of(block * tq + g * qb, qb)
        qt = q_ref[pl.ds(r0, qb), :].astype(F32).T
        wq_ref[g] = jnp.concatenate([jnp.where(lane < DA_HEAD_DIM, qt, 0.0),
                                     jnp.where(lane >= DA_HEAD_DIM, qt, 0.0)], axis=1).astype(BF16)

    def reset(g):
        m_ref[g] = jnp.full((SUBLANES, 2 * qb), -1e30, F32)
        acc_ref[g] = jnp.zeros((V_ROWS, 2 * qb), F32)

    def finalize(block, g):
        acc = acc_ref[g]
        o0 = acc[0:DA_V_DIM, 0:qb] / acc[DA_V_DIM:DA_V_DIM + 1, 0:qb]
        o1 = acc[0:DA_V_DIM, qb:2 * qb] / acc[DA_V_DIM:DA_V_DIM + 1, qb:2 * qb]
        o = o0 - lam * o1
        o = o * lax.rsqrt(jnp.mean(o * o, axis=0, keepdims=True) + SUBLN_EPS)
        r0 = pl.multiple_of(block * tq + g * qb, qb)
        o_ref[pl.ds(r0, qb), :] = (o * g_ref[...]).T

    all_groups = tuple(range(n_g))

    def scores(tile, buf, groups=all_groups):
        c0 = pl.multiple_of(tile * tk, tk)
        kt = k_ref[pl.ds(c0, tk), :]
        for g in groups:
            s = jnp.dot(kt, wq_ref[g], preferred_element_type=F32)
            s_ref[buf, g] = s
            tmax_ref[buf, g] = jnp.broadcast_to(jnp.max(s, axis=0, keepdims=True), (SUBLANES, 2 * qb))

    def softmax_pv(tile, buf, groups=all_groups):
        c0 = pl.multiple_of(tile * tk, tk)
        vt = vt_ref[:, pl.ds(c0, tk)]
        for g in groups:
            m_old = m_ref[g]
            m_new = jnp.maximum(m_old, tmax_ref[buf, g])
            m_ref[g] = m_new
            p = jnp.exp2(s_ref[buf, g] - m_new[0:1, :]).astype(BF16)
            pv = jnp.dot(vt, p, preferred_element_type=F32)
            acc_ref[g] = acc_ref[g] * jnp.exp2(m_old - m_new)[0:1, :] + pv

    def step(tile, parity):
        for g in all_groups:
            scores(tile + 1, 1 - parity, groups=(g,))
            softmax_pv(tile, parity, groups=(g,))

    assert unroll % 2 == 0 and n_t % 2 == 0
    n_loop = (n_t - 1) // unroll

    def kv_body(jj, carry):
        for u in range(unroll):
            step(jj * unroll + u, u % 2)
        return carry

    for g in all_groups:
        build_query_weights(0, g)
        reset(g)
    scores(0, 0)

    def query_block(qi, carry):
        lax.fori_loop(0, n_loop, kv_body, 0)
        for t in range(n_loop * unroll, n_t - 1):
            step(t, t % 2)
        next_block = jnp.minimum(qi + 1, n_q - 1)
        for g in all_groups:
            build_query_weights(next_block, g)
            scores(0, 0, groups=(g,))
            softmax_pv(n_t - 1, 1, groups=(g,))
            finalize(qi, g)
            reset(g)
        return carry

    lax.fori_loop(0, n_q, query_block, 0)


def _diff_attention(q, k, vt, lam_arr, g_mat, tq, tk, unroll):
    B, S, _ = q.shape
    n_g = tq // LANES
    kern = functools.partial(_attn_kernel, tq=tq, tk=tk, qb=LANES, unroll=unroll)
    head_block = pl.BlockSpec((None, S, LANES), lambda b_, h: (b_, 0, h))
    return pl.pallas_call(
        kern,
        grid=(B, DA_HEADS),
        in_specs=[pl.BlockSpec(memory_space=pltpu.SMEM),
                  head_block,
                  head_block,
                  pl.BlockSpec((None, None, V_ROWS, S), lambda b_, h: (b_, h, 0, 0)),
                  pl.BlockSpec((DA_V_DIM, LANES), lambda b_, h: (0, 0))],
        out_specs=head_block,
        out_shape=jax.ShapeDtypeStruct((B, S, DA_WIDTH), F32),
        scratch_shapes=[pltpu.VMEM((n_g, LANES, 2 * LANES), BF16),
                        pltpu.VMEM((n_g, SUBLANES, 2 * LANES), F32),
                        pltpu.VMEM((n_g, V_ROWS, 2 * LANES), F32),
                        pltpu.VMEM((2, n_g, tk, 2 * LANES), F32),
                        pltpu.VMEM((2, n_g, SUBLANES, 2 * LANES), F32)],
        compiler_params=_cparams(("parallel", "parallel")),
        name="diff_attn",
    )(lam_arr, q, k, vt, g_mat)


def _memkv_kernel(m_ref, w_ref, km_ref, vm_ref):
    kv = jnp.dot(m_ref[...].astype(BF16), w_ref[...], preferred_element_type=F32)
    km_ref[...] = kv[:, 0:XA_WIDTH].astype(BF16)
    vm_ref[...] = kv[:, XA_WIDTH:2 * XA_WIDTH].astype(BF16)


def _mem_kv(mem, w_mem_kv):
    B, M, D = mem.shape
    return pl.pallas_call(
        _memkv_kernel,
        grid=(B,),
        in_specs=[pl.BlockSpec((None, M, D), lambda b_: (b_, 0, 0)),
                  _const_spec((D, 2 * XA_WIDTH))],
        out_specs=[pl.BlockSpec((None, M, XA_WIDTH), lambda b_: (b_, 0, 0)),
                   pl.BlockSpec((None, M, XA_WIDTH), lambda b_: (b_, 0, 0))],
        out_shape=[jax.ShapeDtypeStruct((B, M, XA_WIDTH), BF16),
                   jax.ShapeDtypeStruct((B, M, XA_WIDTH), BF16)],
        compiler_params=_cparams(("parallel",)),
        name="mem_kv",
    )(mem, w_mem_kv)


def _silu(z):
    return z * (1.0 / (1.0 + jnp.exp(-z)))


def _sigmoid(z):
    return 1.0 / (1.0 + jnp.exp(-z))


def _merge_kernel(x_ref, pu_ref, pup_ref, pun_ref, o_ref, km_ref, vm_ref, wg_ref, pw_ref, ps_ref,
                  wa_ref, wb_ref, wc_ref, wo_ref, g_ref, b_ref, y_ref, ue_ref, *, seq_len, alpha):
    i = pl.program_id(1)
    n_i = pl.num_programs(1)
    tm = x_ref.shape[0]
    x = x_ref[...]
    xb = x.astype(BF16)

    u = pu_ref[...]
    ue_ref[0:POOL_HALO, :] = jnp.where(i > 0, pup_ref[...], 0.0)
    ue_ref[POOL_HALO:POOL_HALO + tm, :] = u
    ue_ref[POOL_HALO + tm:2 * POOL_HALO + tm, :] = jnp.where(i < n_i - 1, pun_ref[...], 0.0)
    t = (lax.broadcasted_iota(jnp.int32, (tm, POOL_GROUP_WIDTH), 0) + i * tm).astype(F32)
    pool_d = []
    for g, w in enumerate(POOL_WINDOWS):
        cols = slice(g * POOL_GROUP_WIDTH, (g + 1) * POOL_GROUP_WIDTH)
        part, span = ue_ref[:, cols], 1
        while span < w:
            n = part.shape[0] - span
            part, span = part[0:n] + part[span:span + n], 2 * span
        r0 = POOL_HALO - w // 2
        wsum = part[r0:r0 + tm]
        cnt = w - jnp.maximum(w // 2 - t, 0.0) - jnp.maximum(t - (seq_len - w // 2), 0.0)
        pool_d.append((wsum / cnt - u[:, cols]).astype(BF16))

    def gate(c0, c1):
        return jnp.dot(xb, wg_ref[:, c0:c1], preferred_element_type=F32)

    xq = gate(1536, 2048).astype(BF16)
    sc = []
    for h in range(XA_HEADS):
        cols = slice(h * XA_HEAD_DIM, (h + 1) * XA_HEAD_DIM)
        sc.append(lax.dot_general(xq[:, cols], km_ref[:, cols], (((1,), (1,)), ((), ())),
                                  preferred_element_type=F32) * (XA_HEAD_DIM ** -0.5))

    yb = o_ref[...] * _silu(gate(512, 1536))
    yb = jnp.dot(yb.astype(BF16), wb_ref[...], preferred_element_type=F32)
    z_pool = gate(0, 512)
    z_mem = gate(2048, 2560)

    yc_parts = []
    for h in range(XA_HEADS):
        cols = slice(h * XA_HEAD_DIM, (h + 1) * XA_HEAD_DIM)
        e = jnp.exp(sc[h] - jnp.max(sc[h], axis=-1, keepdims=True))
        pm = (e / jnp.sum(e, axis=-1, keepdims=True)).astype(BF16)
        yc_parts.append(jnp.dot(pm, vm_ref[:, cols], preferred_element_type=F32))
    yc = jnp.concatenate(yc_parts, axis=1) * _silu(z_mem)
    g_a = _sigmoid(gate(2560, 3584))
    yc = jnp.dot(yc.astype(BF16), wc_ref[...], preferred_element_type=F32)

    ya = jnp.concatenate([jnp.dot(pool_d[g], pw_ref[g], preferred_element_type=F32)
                          for g in range(POOL_GROUPS)], axis=1) * ps_ref[...] * _silu(z_pool)
    g_b = _sigmoid(gate(3584, 4608))
    ya = jnp.dot(ya.astype(BF16), wa_ref[...], preferred_element_type=F32)

    merged = (g_a * ya + g_b * yb + _sigmoid(gate(4608, 5632)) * yc).astype(BF16)
    for r in (slice(0, tm // 2), slice(tm // 2, tm)):
        out = jnp.dot(merged[r], wo_ref[...], preferred_element_type=F32)
        y_ref[r, :] = _ln_rows(alpha * x[r] + out, g_ref[...], b_ref[...])


def _merge(x, pu, o, km, vm, w_g, pool_w, pool_scale, w_a, w_b, w_c, w_o, ln_g, ln_b, tm, alpha):
    B, S, D = x.shape
    hb = tm // POOL_HALO
    n_hb = S // POOL_HALO
    kern = functools.partial(_merge_kernel, seq_len=S, alpha=alpha)
    tile = lambda w: pl.BlockSpec((None, tm, w), lambda b_, i: (b_, i, 0))
    return pl.pallas_call(
        kern,
        grid=(B, S // tm),
        in_specs=[tile(D),
                  tile(POOL_WIDTH),
                  pl.BlockSpec((None, POOL_HALO, POOL_WIDTH), lambda b_, i: (b_, jnp.maximum(i * hb - 1, 0), 0)),
                  pl.BlockSpec((None, POOL_HALO, POOL_WIDTH),
                               lambda b_, i: (b_, jnp.minimum((i + 1) * hb, n_hb - 1), 0)),
                  tile(DA_WIDTH),
                  pl.BlockSpec((None, N_MEM, XA_WIDTH), lambda b_, i: (b_, 0, 0)),
                  pl.BlockSpec((None, N_MEM, XA_WIDTH), lambda b_, i: (b_, 0, 0)),
                  _const_spec(w_g.shape),
                  _const_spec(pool_w.shape),
                  _const_spec((1, POOL_WIDTH)),
                  _const_spec(w_a.shape), _const_spec(w_b.shape), _const_spec(w_c.shape), _const_spec(w_o.shape),
                  _const_spec((1, D)), _const_spec((1, D))],
        out_specs=tile(D),
        out_shape=jax.ShapeDtypeStruct((B, S, D), F32),
        scratch_shapes=[pltpu.VMEM((tm + 2 * POOL_HALO, POOL_WIDTH), F32)],
        compiler_params=_cparams(("parallel", "arbitrary")),
        name="merge",
    )(x, pu, pu, pu, o, km, vm, w_g, pool_w, pool_scale.reshape(1, POOL_WIDTH), w_a, w_b, w_c, w_o,
      ln_g.reshape(1, D), ln_b.reshape(1, D))


def _rope_tables(seq):
    pos = jnp.arange(seq, dtype=F32)
    inv = ROPE_THETA ** (-jnp.arange(0, ROT_DIM, 2, dtype=F32) / ROT_DIM)
    ang = pos[:, None] * inv[None, :]
    cos, sin = jnp.cos(ang), jnp.sin(ang)
    half = ROT_DIM // 2
    lane = jnp.arange(LANES) % DA_HEAD_DIM
    idx = lane % half
    c = jnp.where(lane[None, :] < ROT_DIM, cos[:, idx], 1.0)
    sa = jnp.where((lane[None, :] >= half) & (lane[None, :] < ROT_DIM), sin[:, idx], 0.0)
    sb = jnp.where(lane[None, :] < half, -sin[:, idx], 0.0)
    return c.astype(F32), sa.astype(F32), sb.astype(F32)


def _tiles(S):
    if S >= 8192:
        attn = dict(tq=1024, tk=512, unroll=4)
    else:
        attn = dict(tq=2048, tk=512, unroll=2)
    return dict(tm_ln=512, tm_proj=512, tm_merge=512, **attn)


def _trunk(x, mem, ln_in_g, ln_in_b, layers):
    B, S, _ = x.shape
    depth = len(layers)
    alpha = (2.0 * depth) ** 0.25
    t = _tiles(S)
    rope_c, rope_sa, rope_sb = _rope_tables(S)
    x = _layernorm(x, ln_in_g, ln_in_b, t["tm_ln"])
    for L in layers:
        pu, q, k, vt = _project(x, L["w_a"], rope_c, rope_sa, rope_sb, t["tm_proj"])
        o = _diff_attention(q, k, vt, L["lam"], L["subln"], t["tq"], t["tk"], t["unroll"])
        km, vm = _mem_kv(mem, L["w_mem_kv"])
        x = _merge(x, pu, o, km, vm, L["w_g"], L["pool_w"], L["pool_scale"], L["w_br_a"], L["w_br_b"],
                   L["w_br_c"], L["w_out"], L["ln_g"], L["ln_b"], t["tm_merge"], alpha)
    return x


def _prepare_layers(w_in, w_mem_kv, pool_w, pool_scale, lam_q1, lam_k1, lam_q2, lam_k2, subln_g,
                    w_br_a, w_br_b, w_br_c, w_out, ln_g, ln_b):
    layers = []
    for i in range(w_in.shape[0]):
        lam_init = 0.8 - 0.6 * math.exp(-0.3 * i)
        lam = (jnp.exp(jnp.sum(lam_q1[i] * lam_k1[i])) - jnp.exp(jnp.sum(lam_q2[i] * lam_k2[i])) + lam_init)
        wi = w_in[i].astype(BF16)
        layers.append(dict(
            w_a=jnp.concatenate([wi[:, C_PU:C_PZ], wi[:, C_Q:C_AZ]], axis=1),
            w_g=jnp.concatenate([wi[:, C_PZ:C_Q], wi[:, C_AZ:C_END]], axis=1),
            w_mem_kv=w_mem_kv[i].astype(BF16),
            pool_w=pool_w[i].astype(BF16),
            pool_scale=pool_scale[i],
            lam=jnp.reshape(lam, (1,)).astype(F32),
            subln=jnp.broadcast_to((subln_g[i] * (1.0 - lam_init))[:, None], (DA_V_DIM, LANES)).astype(F32),
            w_br_a=w_br_a[i].astype(BF16), w_br_b=w_br_b[i].astype(BF16), w_br_c=w_br_c[i].astype(BF16),
            w_out=w_out[i].astype(BF16), ln_g=ln_g[i], ln_b=ln_b[i]))
    return layers


def kernel(x_prompt, x_sample, mem_prompt, mem_sample, ln_in_g, ln_in_b, w_in, w_mem_kv, pool_w, pool_scale,
           lam_q1, lam_k1, lam_q2, lam_k2, subln_g, w_br_a, w_br_b, w_br_c, w_out, ln_g, ln_b):
    layers = _prepare_layers(w_in, w_mem_kv, pool_w, pool_scale, lam_q1, lam_k1, lam_q2, lam_k2, subln_g,
                             w_br_a, w_br_b, w_br_c, w_out, ln_g, ln_b)
    y_prompt = _trunk(x_prompt, mem_prompt, ln_in_g, ln_in_b, layers)
    y_sample = _trunk(x_sample, mem_sample, ln_in_g, ln_in_b, layers)
    return (y_prompt, y_sample)
```

```python
import functools
import math

import jax
import jax.numpy as jnp
from jax import lax
from jax.experimental import pallas as pl
from jax.experimental.pallas import tpu as pltpu

F32 = jnp.float32
BF16 = jnp.bfloat16

D_MODEL = 1024
POOL_GROUPS = 4
POOL_GROUP_WIDTH = 128
POOL_WIDTH = POOL_GROUPS * POOL_GROUP_WIDTH
POOL_WINDOWS = (2, 4, 8, 16)
POOL_HALO = 8
DA_HEADS = 8
DA_HEAD_DIM = 64
DA_V_DIM = 2 * DA_HEAD_DIM
DA_WIDTH = DA_HEADS * DA_V_DIM
ROT_DIM = DA_HEAD_DIM // 4
ROPE_THETA = 500000.0
SUBLN_EPS = 1e-5
N_MEM = 256
XA_HEADS = 4
XA_HEAD_DIM = 128
XA_WIDTH = XA_HEADS * XA_HEAD_DIM
LN_EPS = 1e-5
LOG2E = 1.4426950408889634

C_PU, C_PZ, C_Q, C_K, C_V, C_AZ, C_XQ, C_XZ, C_GL, C_END = 0, 512, 1024, 2048, 3072, 4096, 5120, 5632, 6144, 9216
G_PZ = 0
G_AZ = G_PZ + (C_Q - C_PZ)
G_XQ = G_AZ + (C_XQ - C_AZ)
G_XZ = G_XQ + (C_XZ - C_XQ)
G_GA = G_XZ + (C_GL - C_XZ)
G_GB = G_GA + D_MODEL
G_GC = G_GB + D_MODEL
G_END = G_GC + D_MODEL

V_ROWS = DA_V_DIM + 16
LANES = 128
SUBLANES = 8
VMEM_LIMIT = 56 * 1024 * 1024


def _cparams(sem):
    return pltpu.CompilerParams(dimension_semantics=sem, vmem_limit_bytes=VMEM_LIMIT)


def _const_spec(shape):
    n = len(shape)
    return pl.BlockSpec(shape, lambda *_: (0,) * n, pipeline_mode=pl.Buffered(1))


def _ln_rows(x, g, b):
    mu = jnp.mean(x, axis=-1, keepdims=True)
    xc = x - mu
    var = jnp.mean(xc * xc, axis=-1, keepdims=True)
    return xc * lax.rsqrt(var + LN_EPS) * g + b


def _ln_kernel(x_ref, g_ref, b_ref, y_ref):
    y_ref[...] = _ln_rows(x_ref[...], g_ref[...], b_ref[...])


def _layernorm(x, g, b, tm):
    B, S, D = x.shape
    return pl.pallas_call(
        _ln_kernel,
        grid=(B, S // tm),
        in_specs=[pl.BlockSpec((None, tm, D), lambda b_, i: (b_, i, 0)),
                  _const_spec((1, D)), _const_spec((1, D))],
        out_specs=pl.BlockSpec((None, tm, D), lambda b_, i: (b_, i, 0)),
        out_shape=jax.ShapeDtypeStruct((B, S, D), F32),
        compiler_params=_cparams(("parallel", "parallel")),
        name="ln_in",
    )(x, g.reshape(1, D), b.reshape(1, D))


def _proj_kernel(x_ref, w_ref, c_ref, sa_ref, sb_ref, pu_ref, q_ref, k_ref, vt_ref):
    x = x_ref[...].astype(BF16)
    tm = x.shape[0]
    pu_ref[...] = jnp.dot(x, w_ref[:, 0:POOL_WIDTH], preferred_element_type=F32)
    qk = jnp.dot(x, w_ref[:, POOL_WIDTH:POOL_WIDTH + 2 * DA_WIDTH], preferred_element_type=F32)
    c = c_ref[...]
    sa = sa_ref[...]
    sb = sb_ref[...]
    q_scale = (DA_HEAD_DIM ** -0.5) * LOG2E
    for h in range(2 * DA_HEADS):
        blk = qk[:, h * LANES:(h + 1) * LANES]
        rot = blk * c + pltpu.roll(blk, 8, axis=1) * sa + pltpu.roll(blk, LANES - 8, axis=1) * sb
        if h < DA_HEADS:
            q_ref[:, h * LANES:(h + 1) * LANES] = (rot * q_scale).astype(BF16)
        else:
            hh = h - DA_HEADS
            k_ref[:, hh * LANES:(hh + 1) * LANES] = rot.astype(BF16)
    v = jnp.dot(x, w_ref[:, POOL_WIDTH + 2 * DA_WIDTH:POOL_WIDTH + 3 * DA_WIDTH], preferred_element_type=F32)
    ones = jnp.ones((V_ROWS - DA_V_DIM, tm), BF16)
    for h in range(DA_HEADS):
        vt_ref[h, 0:DA_V_DIM, :] = v[:, h * LANES:(h + 1) * LANES].T.astype(BF16)
        vt_ref[h, DA_V_DIM:V_ROWS, :] = ones


def _project(x, w_a, rope_c, rope_sa, rope_sb, tm):
    B, S, D = x.shape
    n_a = w_a.shape[1]
    return pl.pallas_call(
        _proj_kernel,
        grid=(B, S // tm),
        in_specs=[pl.BlockSpec((None, tm, D), lambda b_, i: (b_, i, 0)),
                  _const_spec((D, n_a)),
                  pl.BlockSpec((tm, LANES), lambda b_, i: (i, 0)),
                  pl.BlockSpec((tm, LANES), lambda b_, i: (i, 0)),
                  pl.BlockSpec((tm, LANES), lambda b_, i: (i, 0))],
        out_specs=[pl.BlockSpec((None, tm, POOL_WIDTH), lambda b_, i: (b_, i, 0)),
                   pl.BlockSpec((None, tm, DA_WIDTH), lambda b_, i: (b_, i, 0)),
                   pl.BlockSpec((None, tm, DA_WIDTH), lambda b_, i: (b_, i, 0)),
                   pl.BlockSpec((None, DA_HEADS, V_ROWS, tm), lambda b_, i: (b_, 0, 0, i))],
        out_shape=[jax.ShapeDtypeStruct((B, S, POOL_WIDTH), F32),
                   jax.ShapeDtypeStruct((B, S, DA_WIDTH), BF16),
                   jax.ShapeDtypeStruct((B, S, DA_WIDTH), BF16),
                   jax.ShapeDtypeStruct((B, DA_HEADS, V_ROWS, S), BF16)],
        compiler_params=_cparams(("parallel", "parallel")),
        name="proj_qkv",
    )(x, w_a, rope_c, rope_sa, rope_sb)


def _attn_kernel(lam_ref, q_ref, k_ref, vt_ref, g_ref, o_ref, wq_ref, m_ref, acc_ref, s_ref, tmax_ref,
                 *, tq, tk, qb, unroll):
    S = k_ref.shape[0]
    n_q = S // tq
    n_g = tq // qb
    n_t = S // tk
    lam = lam_ref[0]
    lane = lax.broadcasted_iota(jnp.int32, (LANES, qb), 0)

    def build_query_weights(block, g):
        r0 = pl.multiple_of(block * tq + g * qb, qb)
        qt = q_ref[pl.ds(r0, qb), :].astype(F32).T
        wq_ref[g] = jnp.concatenate([jnp.where(lane < DA_HEAD_DIM, qt, 0.0),
                                     jnp.where(lane >= DA_HEAD_DIM, qt, 0.0)], axis=1).astype(BF16)

    def reset(g):
        m_ref[g] = jnp.full((SUBLANES, 2 * qb), -1e30, F32)
        acc_ref[g] = jnp.zeros((V_ROWS, 2 * qb), F32)

    def finalize(block, g):
        acc = acc_ref[g]
        o0 = acc[0:DA_V_DIM, 0:qb] / acc[DA_V_DIM:DA_V_DIM + 1, 0:qb]
        o1 = acc[0:DA_V_DIM, qb:2 * qb] / acc[DA_V_DIM:DA_V_DIM + 1, qb:2 * qb]
        o = o0 - lam * o1
        o = o * lax.rsqrt(jnp.mean(o * o, axis=0, keepdims=True) + SUBLN_EPS)
        r0 = pl.multiple_of(block * tq + g * qb, qb)
        o_ref[pl.ds(r0, qb), :] = (o * g_ref[...]).T

    all_groups = tuple(range(n_g))

    def scores(tile, buf, groups=all_groups):
        c0 = pl.multiple_of(tile * tk, tk)
        kt = k_ref[pl.ds(c0, tk), :]
        for g in groups:
            s = jnp.dot(kt, wq_ref[g], preferred_element_type=F32)
            s_ref[buf, g] = s
            tmax_ref[buf, g] = jnp.broadcast_to(jnp.max(s, axis=0, keepdims=True), (SUBLANES, 2 * qb))

    def softmax_pv(tile, buf, groups=all_groups):
        c0 = pl.multiple_of(tile * tk, tk)
        vt = vt_ref[:, pl.ds(c0, tk)]
        for g in groups:
            m_old = m_ref[g]
            m_new = jnp.maximum(m_old, tmax_ref[buf, g])
            m_ref[g] = m_new
            p = jnp.exp2(s_ref[buf, g] - m_new[0:1, :]).astype(BF16)
            pv = jnp.dot(vt, p, preferred_element_type=F32)
            acc_ref[g] = acc_ref[g] * jnp.exp2(m_old - m_new)[0:1, :] + pv

    def step(tile, parity):
        for g in all_groups:
            scores(tile + 1, 1 - parity, groups=(g,))
            softmax_pv(tile, parity, groups=(g,))

    assert unroll % 2 == 0 and n_t % 2 == 0
    n_loop = (n_t - 1) // unroll

    def kv_body(jj, carry):
        for u in range(unroll):
            step(jj * unroll + u, u % 2)
        return carry

    for g in all_groups:
        build_query_weights(0, g)
        reset(g)
    scores(0, 0)

    def query_block(qi, carry):
        lax.fori_loop(0, n_loop, kv_body, 0)
        for t in range(n_loop * unroll, n_t - 1):
            step(t, t % 2)
        next_block = jnp.minimum(qi + 1, n_q - 1)
        for g in all_groups:
            build_query_weights(next_block, g)
            scores(0, 0, groups=(g,))
            softmax_pv(n_t - 1, 1, groups=(g,))
            finalize(qi, g)
            reset(g)
        return carry

    lax.fori_loop(0, n_q, query_block, 0)


def _diff_attention(q, k, vt, lam_arr, g_mat, tq, tk, unroll):
    B, S, _ = q.shape
    n_g = tq // LANES
    kern = functools.partial(_attn_kernel, tq=tq, tk=tk, qb=LANES, unroll=unroll)
    head_block = pl.BlockSpec((None, S, LANES), lambda b_, h: (b_, 0, h))
    return pl.pallas_call(
        kern,
        grid=(B, DA_HEADS),
        in_specs=[pl.BlockSpec(memory_space=pltpu.SMEM),
                  head_block,
                  head_block,
                  pl.BlockSpec((None, None, V_ROWS, S), lambda b_, h: (b_, h, 0, 0)),
                  pl.BlockSpec((DA_V_DIM, LANES), lambda b_, h: (0, 0))],
        out_specs=head_block,
        out_shape=jax.ShapeDtypeStruct((B, S, DA_WIDTH), F32),
        scratch_shapes=[pltpu.VMEM((n_g, LANES, 2 * LANES), BF16),
                        pltpu.VMEM((n_g, SUBLANES, 2 * LANES), F32),
                        pltpu.VMEM((n_g, V_ROWS, 2 * LANES), F32),
                        pltpu.VMEM((2, n_g, tk, 2 * LANES), F32),
                        pltpu.VMEM((2, n_g, SUBLANES, 2 * LANES), F32)],
        compiler_params=_cparams(("parallel", "parallel")),
        name="diff_attn",
    )(lam_arr, q, k, vt, g_mat)


def _memkv_kernel(m_ref, w_ref, km_ref, vm_ref):
    kv = jnp.dot(m_ref[...].astype(BF16), w_ref[...], preferred_element_type=F32)
    km_ref[...] = kv[:, 0:XA_WIDTH].astype(BF16)
    vm_ref[...] = kv[:, XA_WIDTH:2 * XA_WIDTH].astype(BF16)


def _mem_kv(mem, w_mem_kv):
    B, M, D = mem.shape
    return pl.pallas_call(
        _memkv_kernel,
        grid=(B,),
        in_specs=[pl.BlockSpec((None, M, D), lambda b_: (b_, 0, 0)),
                  _const_spec((D, 2 * XA_WIDTH))],
        out_specs=[pl.BlockSpec((None, M, XA_WIDTH), lambda b_: (b_, 0, 0)),
                   pl.BlockSpec((None, M, XA_WIDTH), lambda b_: (b_, 0, 0))],
        out_shape=[jax.ShapeDtypeStruct((B, M, XA_WIDTH), BF16),
                   jax.ShapeDtypeStruct((B, M, XA_WIDTH), BF16)],
        compiler_params=_cparams(("parallel",)),
        name="mem_kv",
    )(mem, w_mem_kv)


def _silu(z):
    return z * (1.0 / (1.0 + jnp.exp(-z)))


def _sigmoid(z):
    return 1.0 / (1.0 + jnp.exp(-z))


def _merge_kernel(x_ref, pu_ref, pup_ref, pun_ref, o_ref, km_ref, vm_ref, wg_ref, pw_ref, ps_ref,
                  wa_ref, wb_ref, wc_ref, wo_ref, g_ref, b_ref, y_ref, ue_ref, *, seq_len, alpha):
    i = pl.program_id(1)
    n_i = pl.num_programs(1)
    tm = x_ref.shape[0]
    x = x_ref[...]
    xb = x.astype(BF16)

    u = pu_ref[...]
    ue_ref[0:POOL_HALO, :] = jnp.where(i > 0, pup_ref[...], 0.0)
    ue_ref[POOL_HALO:POOL_HALO + tm, :] = u
    ue_ref[POOL_HALO + tm:2 * POOL_HALO + tm, :] = jnp.where(i < n_i - 1, pun_ref[...], 0.0)
    row = lax.broadcasted_iota(jnp.int32, (POOL_HALO, POOL_GROUP_WIDTH), 0) + i * tm
    t_head = row.astype(F32)
    t_tail = (row + (tm - POOL_HALO)).astype(F32)

    def window_rows(t, w):
        return w - jnp.maximum(w // 2 - t, 0.0) - jnp.maximum(t - (seq_len - w // 2), 0.0)

    pool_d = []
    for g, w in enumerate(POOL_WINDOWS):
        assert w & (w - 1) == 0 and w // 2 <= POOL_HALO
        cols = slice(g * POOL_GROUP_WIDTH, (g + 1) * POOL_GROUP_WIDTH)
        part, span = ue_ref[:, cols], 1
        while span < w:
            n = part.shape[0] - span
            part, span = part[0:n] + part[span:span + n], 2 * span
        r0 = POOL_HALO - w // 2
        wsum = part[r0:r0 + tm]
        mean = jnp.concatenate([wsum[0:POOL_HALO] / window_rows(t_head, w),
                                wsum[POOL_HALO:tm - POOL_HALO] * (1.0 / w),
                                wsum[tm - POOL_HALO:tm] / window_rows(t_tail, w)], axis=0)
        pool_d.append((mean - u[:, cols]).astype(BF16))

    def gate(c0, c1):
        return jnp.dot(xb, wg_ref[:, c0:c1], preferred_element_type=F32)

    xq = gate(G_XQ, G_XZ).astype(BF16)
    sc = []
    for h in range(XA_HEADS):
        cols = slice(h * XA_HEAD_DIM, (h + 1) * XA_HEAD_DIM)
        sc.append(lax.dot_general(xq[:, cols], km_ref[:, cols], (((1,), (1,)), ((), ())),
                                  preferred_element_type=F32) * (XA_HEAD_DIM ** -0.5))

    yb = o_ref[...] * _silu(gate(G_AZ, G_XQ))
    yb = jnp.dot(yb.astype(BF16), wb_ref[...], preferred_element_type=F32)
    z_pool = gate(G_PZ, G_AZ)
    z_mem = gate(G_XZ, G_GA)

    yc_parts = []
    for h in range(XA_HEADS):
        cols = slice(h * XA_HEAD_DIM, (h + 1) * XA_HEAD_DIM)
        e = jnp.exp(sc[h] - jnp.max(sc[h], axis=-1, keepdims=True))
        pm = (e / jnp.sum(e, axis=-1, keepdims=True)).astype(BF16)
        yc_parts.append(jnp.dot(pm, vm_ref[:, cols], preferred_element_type=F32))
    yc = jnp.concatenate(yc_parts, axis=1) * _silu(z_mem)
    g_a = _sigmoid(gate(G_GA, G_GB))
    yc = jnp.dot(yc.astype(BF16), wc_ref[...], preferred_element_type=F32)

    ya = jnp.concatenate([jnp.dot(pool_d[g], pw_ref[g], preferred_element_type=F32)
                          for g in range(POOL_GROUPS)], axis=1) * ps_ref[...] * _silu(z_pool)
    g_b = _sigmoid(gate(G_GB, G_GC))
    ya = jnp.dot(ya.astype(BF16), wa_ref[...], preferred_element_type=F32)

    merged = (g_a * ya + g_b * yb + _sigmoid(gate(G_GC, G_END)) * yc).astype(BF16)
    for r in (slice(0, tm // 2), slice(tm // 2, tm)):
        out = jnp.dot(merged[r], wo_ref[...], preferred_element_type=F32)
        y_ref[r, :] = _ln_rows(alpha * x[r] + out, g_ref[...], b_ref[...])


def _merge(x, pu, o, km, vm, w_g, pool_w, pool_scale, w_a, w_b, w_c, w_o, ln_g, ln_b, tm, alpha):
    B, S, D = x.shape
    hb = tm // POOL_HALO
    n_hb = S // POOL_HALO
    kern = functools.partial(_merge_kernel, seq_len=S, alpha=alpha)
    tile = lambda w: pl.BlockSpec((None, tm, w), lambda b_, i: (b_, i, 0))
    return pl.pallas_call(
        kern,
        grid=(B, S // tm),
        in_specs=[tile(D),
                  tile(POOL_WIDTH),
                  pl.BlockSpec((None, POOL_HALO, POOL_WIDTH), lambda b_, i: (b_, jnp.maximum(i * hb - 1, 0), 0)),
                  pl.BlockSpec((None, POOL_HALO, POOL_WIDTH),
                               lambda b_, i: (b_, jnp.minimum((i + 1) * hb, n_hb - 1), 0)),
                  tile(DA_WIDTH),
                  pl.BlockSpec((None, N_MEM, XA_WIDTH), lambda b_, i: (b_, 0, 0)),
                  pl.BlockSpec((None, N_MEM, XA_WIDTH), lambda b_, i: (b_, 0, 0)),
                  _const_spec(w_g.shape),
                  _const_spec(pool_w.shape),
                  _const_spec((1, POOL_WIDTH)),
                  _const_spec(w_a.shape), _const_spec(w_b.shape), _const_spec(w_c.shape), _const_spec(w_o.shape),
                  _const_spec((1, D)), _const_spec((1, D))],
        out_specs=tile(D),
        out_shape=jax.ShapeDtypeStruct((B, S, D), F32),
        scratch_shapes=[pltpu.VMEM((tm + 2 * POOL_HALO, POOL_WIDTH), F32)],
        compiler_params=_cparams(("parallel", "arbitrary")),
        name="merge",
    )(x, pu, pu, pu, o, km, vm, w_g, pool_w, pool_scale.reshape(1, POOL_WIDTH), w_a, w_b, w_c, w_o,
      ln_g.reshape(1, D), ln_b.reshape(1, D))


def _rope_tables(seq):
    pos = jnp.arange(seq, dtype=F32)
    inv = ROPE_THETA ** (-jnp.arange(0, ROT_DIM, 2, dtype=F32) / ROT_DIM)
    ang = pos[:, None] * inv[None, :]
    cos, sin = jnp.cos(ang), jnp.sin(ang)
    half = ROT_DIM // 2
    lane = jnp.arange(LANES) % DA_HEAD_DIM
    idx = lane % half
    c = jnp.where(lane[None, :] < ROT_DIM, cos[:, idx], 1.0)
    sa = jnp.where((lane[None, :] >= half) & (lane[None, :] < ROT_DIM), sin[:, idx], 0.0)
    sb = jnp.where(lane[None, :] < half, -sin[:, idx], 0.0)
    return c.astype(F32), sa.astype(F32), sb.astype(F32)


def _tiles(S):
    if S >= 8192:
        attn = dict(tq=1024, tk=512, unroll=4)
    else:
        attn = dict(tq=2048, tk=512, unroll=2)
    return dict(tm_ln=512, tm_proj=512, tm_merge=512, **attn)


def _trunk(x, mem, ln_in_g, ln_in_b, layers):
    B, S, _ = x.shape
    depth = len(layers)
    alpha = (2.0 * depth) ** 0.25
    t = _tiles(S)
    rope_c, rope_sa, rope_sb = _rope_tables(S)
    x = _layernorm(x, ln_in_g, ln_in_b, t["tm_ln"])
    for L in layers:
        pu, q, k, vt = _project(x, L["w_a"], rope_c, rope_sa, rope_sb, t["tm_proj"])
        o = _diff_attention(q, k, vt, L["lam"], L["subln"], t["tq"], t["tk"], t["unroll"])
        km, vm = _mem_kv(mem, L["w_mem_kv"])
        x = _merge(x, pu, o, km, vm, L["w_g"], L["pool_w"], L["pool_scale"], L["w_br_a"], L["w_br_b"],
                   L["w_br_c"], L["w_out"], L["ln_g"], L["ln_b"], t["tm_merge"], alpha)
    return x


def _prepare_layers(w_in, w_mem_kv, pool_w, pool_scale, lam_q1, lam_k1, lam_q2, lam_k2, subln_g,
                    w_br_a, w_br_b, w_br_c, w_out, ln_g, ln_b):
    layers = []
    for i in range(w_in.shape[0]):
        lam_init = 0.8 - 0.6 * math.exp(-0.3 * i)
        lam = (jnp.exp(jnp.sum(lam_q1[i] * lam_k1[i])) - jnp.exp(jnp.sum(lam_q2[i] * lam_k2[i])) + lam_init)
        wi = w_in[i].astype(BF16)
        layers.append(dict(
            w_a=jnp.concatenate([wi[:, C_PU:C_PZ], wi[:, C_Q:C_AZ]], axis=1),
            w_g=jnp.concatenate([wi[:, C_PZ:C_Q], wi[:, C_AZ:C_END]], axis=1),
            w_mem_kv=w_mem_kv[i].astype(BF16),
            pool_w=pool_w[i].astype(BF16),
            pool_scale=pool_scale[i],
            lam=jnp.reshape(lam, (1,)).astype(F32),
            subln=jnp.broadcast_to((subln_g[i] * (1.0 - lam_init))[:, None], (DA_V_DIM, LANES)).astype(F32),
            w_br_a=w_br_a[i].astype(BF16), w_br_b=w_br_b[i].astype(BF16), w_br_c=w_br_c[i].astype(BF16),
            w_out=w_out[i].astype(BF16), ln_g=ln_g[i], ln_b=ln_b[i]))
    return layers


def kernel(x_prompt, x_sample, mem_prompt, mem_sample, ln_in_g, ln_in_b, w_in, w_mem_kv, pool_w, pool_scale,
           lam_q1, lam_k1, lam_q2, lam_k2, subln_g, w_br_a, w_br_b, w_br_c, w_out, ln_g, ln_b):
    layers = _prepare_layers(w_in, w_mem_kv, pool_w, pool_scale, lam_q1, lam_k1, lam_q2, lam_k2, subln_g,
                             w_br_a, w_br_b, w_br_c, w_out, ln_g, ln_b)
    y_prompt = _trunk(x_prompt, mem_prompt, ln_in_g, ln_in_b, layers)
    y_sample = _trunk(x_sample, mem_sample, ln_in_g, ln_in_b, layers)
    return (y_prompt, y_sample)
```

```python
import functools
import math

import jax
import jax.numpy as jnp
from jax import lax
from jax.experimental import pallas as pl
from jax.experimental.pallas import tpu as pltpu

F32 = jnp.float32
BF16 = jnp.bfloat16

D_MODEL = 1024
POOL_GROUPS = 4
POOL_GROUP_WIDTH = 128
POOL_WIDTH = POOL_GROUPS * POOL_GROUP_WIDTH
POOL_WINDOWS = (2, 4, 8, 16)
POOL_HALO = 8
DA_HEADS = 8
DA_HEAD_DIM = 64
DA_V_DIM = 2 * DA_HEAD_DIM
DA_WIDTH = DA_HEADS * DA_V_DIM
ROT_DIM = DA_HEAD_DIM // 4
ROPE_THETA = 500000.0
SUBLN_EPS = 1e-5
N_MEM = 256
XA_HEADS = 4
XA_HEAD_DIM = 128
XA_WIDTH = XA_HEADS * XA_HEAD_DIM
LN_EPS = 1e-5
LOG2E = 1.4426950408889634

C_PU, C_PZ, C_Q, C_K, C_V, C_AZ, C_XQ, C_XZ, C_GL, C_END = 0, 512, 1024, 2048, 3072, 4096, 5120, 5632, 6144, 9216
G_PZ = 0
G_AZ = G_PZ + (C_Q - C_PZ)
G_XQ = G_AZ + (C_XQ - C_AZ)
G_XZ = G_XQ + (C_XZ - C_XQ)
G_GA = G_XZ + (C_GL - C_XZ)
G_GB = G_GA + D_MODEL
G_GC = G_GB + D_MODEL
G_END = G_GC + D_MODEL

V_ROWS = DA_V_DIM + 16
LANES = 128
SUBLANES = 8
VMEM_LIMIT = 56 * 1024 * 1024


def _cparams(sem):
    return pltpu.CompilerParams(dimension_semantics=sem, vmem_limit_bytes=VMEM_LIMIT)


def _const_spec(shape):
    n = len(shape)
    return pl.BlockSpec(shape, lambda *_: (0,) * n, pipeline_mode=pl.Buffered(1))


def _ln_rows(x, g, b):
    mu = jnp.mean(x, axis=-1, keepdims=True)
    xc = x - mu
    var = jnp.mean(xc * xc, axis=-1, keepdims=True)
    return xc * lax.rsqrt(var + LN_EPS) * g + b


def _ln_kernel(x_ref, g_ref, b_ref, y_ref):
    y_ref[...] = _ln_rows(x_ref[...], g_ref[...], b_ref[...])


def _layernorm(x, g, b, tm):
    B, S, D = x.shape
    return pl.pallas_call(
        _ln_kernel,
        grid=(B, S // tm),
        in_specs=[pl.BlockSpec((None, tm, D), lambda b_, i: (b_, i, 0)),
                  _const_spec((1, D)), _const_spec((1, D))],
        out_specs=pl.BlockSpec((None, tm, D), lambda b_, i: (b_, i, 0)),
        out_shape=jax.ShapeDtypeStruct((B, S, D), F32),
        compiler_params=_cparams(("parallel", "parallel")),
        name="ln_in",
    )(x, g.reshape(1, D), b.reshape(1, D))


def _proj_kernel(x_ref, w_ref, c_ref, sa_ref, sb_ref, pu_ref, q_ref, k_ref, vt_ref):
    x = x_ref[...].astype(BF16)
    tm = x.shape[0]
    pu_ref[...] = jnp.dot(x, w_ref[:, 0:POOL_WIDTH], preferred_element_type=F32)
    qk = jnp.dot(x, w_ref[:, POOL_WIDTH:POOL_WIDTH + 2 * DA_WIDTH], preferred_element_type=F32)
    c = c_ref[...]
    sa = sa_ref[...]
    sb = sb_ref[...]
    q_scale = (DA_HEAD_DIM ** -0.5) * LOG2E
    for h in range(2 * DA_HEADS):
        blk = qk[:, h * LANES:(h + 1) * LANES]
        rot = blk * c + pltpu.roll(blk, 8, axis=1) * sa + pltpu.roll(blk, LANES - 8, axis=1) * sb
        if h < DA_HEADS:
            q_ref[:, h * LANES:(h + 1) * LANES] = (rot * q_scale).astype(BF16)
        else:
            hh = h - DA_HEADS
            k_ref[:, hh * LANES:(hh + 1) * LANES] = rot.astype(BF16)
    v = jnp.dot(x, w_ref[:, POOL_WIDTH + 2 * DA_WIDTH:POOL_WIDTH + 3 * DA_WIDTH], preferred_element_type=F32)
    ones = jnp.ones((V_ROWS - DA_V_DIM, tm), BF16)
    for h in range(DA_HEADS):
        vt_ref[h, 0:DA_V_DIM, :] = v[:, h * LANES:(h + 1) * LANES].T.astype(BF16)
        vt_ref[h, DA_V_DIM:V_ROWS, :] = ones


def _project(x, w_a, rope_c, rope_sa, rope_sb, tm):
    B, S, D = x.shape
    n_a = w_a.shape[1]
    return pl.pallas_call(
        _proj_kernel,
        grid=(B, S // tm),
        in_specs=[pl.BlockSpec((None, tm, D), lambda b_, i: (b_, i, 0)),
                  _const_spec((D, n_a)),
                  pl.BlockSpec((tm, LANES), lambda b_, i: (i, 0)),
                  pl.BlockSpec((tm, LANES), lambda b_, i: (i, 0)),
                  pl.BlockSpec((tm, LANES), lambda b_, i: (i, 0))],
        out_specs=[pl.BlockSpec((None, tm, POOL_WIDTH), lambda b_, i: (b_, i, 0)),
                   pl.BlockSpec((None, tm, DA_WIDTH), lambda b_, i: (b_, i, 0)),
                   pl.BlockSpec((None, tm, DA_WIDTH), lambda b_, i: (b_, i, 0)),
                   pl.BlockSpec((None, DA_HEADS, V_ROWS, tm), lambda b_, i: (b_, 0, 0, i))],
        out_shape=[jax.ShapeDtypeStruct((B, S, POOL_WIDTH), F32),
                   jax.ShapeDtypeStruct((B, S, DA_WIDTH), BF16),
                   jax.ShapeDtypeStruct((B, S, DA_WIDTH), BF16),
                   jax.ShapeDtypeStruct((B, DA_HEADS, V_ROWS, S), BF16)],
        compiler_params=_cparams(("parallel", "parallel")),
        name="proj_qkv",
    )(x, w_a, rope_c, rope_sa, rope_sb)


def _attn_kernel(lam_ref, q_ref, k_ref, vt_ref, g_ref, o_ref, wq_ref, m_ref, acc_ref, s_ref, tmax_ref,
                 *, tq, tk, qb, unroll):
    S = k_ref.shape[0]
    n_q = S // tq
    n_g = tq // qb
    n_t = S // tk
    lam = lam_ref[0]
    lane = lax.broadcasted_iota(jnp.int32, (LANES, qb), 0)

    def build_query_weights(block, g):
        r0 = pl.multiple_of(block * tq + g * qb, qb)
        qt = q_ref[pl.ds(r0, qb), :].astype(F32).T
        wq_ref[g] = jnp.concatenate([jnp.where(lane < DA_HEAD_DIM, qt, 0.0),
                                     jnp.where(lane >= DA_HEAD_DIM, qt, 0.0)], axis=1).astype(BF16)

    def reset(g):
        m_ref[g] = jnp.full((SUBLANES, 2 * qb), -1e30, F32)
        acc_ref[g] = jnp.zeros((V_ROWS, 2 * qb), F32)

    def finalize(block, g):
        acc = acc_ref[g]
        o0 = acc[0:DA_V_DIM, 0:qb] / acc[DA_V_DIM:DA_V_DIM + 1, 0:qb]
        o1 = acc[0:DA_V_DIM, qb:2 * qb] / acc[DA_V_DIM:DA_V_DIM + 1, qb:2 * qb]
        o = o0 - lam * o1
        o = o * lax.rsqrt(jnp.mean(o * o, axis=0, keepdims=True) + SUBLN_EPS)
        r0 = pl.multiple_of(block * tq + g * qb, qb)
        o_ref[pl.ds(r0, qb), :] = (o * g_ref[...]).T

    all_groups = tuple(range(n_g))

    def scores(tile, buf, groups=all_groups):
        c0 = pl.multiple_of(tile * tk, tk)
        kt = k_ref[pl.ds(c0, tk), :]
        for g in groups:
            s = jnp.dot(kt, wq_ref[g], preferred_element_type=F32)
            s_ref[buf, g] = s
            tmax_ref[buf, g] = jnp.broadcast_to(jnp.max(s, axis=0, keepdims=True), (SUBLANES, 2 * qb))

    def softmax_pv(tile, buf, groups=all_groups):
        c0 = pl.multiple_of(tile * tk, tk)
        vt = vt_ref[:, pl.ds(c0, tk)]
        for g in groups:
            m_old = m_ref[g]
            m_new = jnp.maximum(m_old, tmax_ref[buf, g])
            m_ref[g] = m_new
            p = jnp.exp2(s_ref[buf, g] - m_new[0:1, :]).astype(BF16)
            pv = jnp.dot(vt, p, preferred_element_type=F32)
            acc_ref[g] = acc_ref[g] * jnp.exp2(m_old - m_new)[0:1, :] + pv

    def step(tile, parity):
        for g in all_groups:
            scores(tile + 1, 1 - parity, groups=(g,))
            softmax_pv(tile, parity, groups=(g,))

    assert unroll % 2 == 0 and n_t % 2 == 0
    n_loop = (n_t - 1) // unroll

    def kv_body(jj, carry):
        for u in range(unroll):
            step(jj * unroll + u, u % 2)
        return carry

    for g in all_groups:
        build_query_weights(0, g)
        reset(g)
    scores(0, 0)

    def query_block(qi, carry):
        lax.fori_loop(0, n_loop, kv_body, 0)
        for t in range(n_loop * unroll, n_t - 1):
            step(t, t % 2)
        next_block = jnp.minimum(qi + 1, n_q - 1)
        for g in all_groups:
            build_query_weights(next_block, g)
            scores(0, 0, groups=(g,))
            softmax_pv(n_t - 1, 1, groups=(g,))
            finalize(qi, g)
            reset(g)
        return carry

    lax.fori_loop(0, n_q, query_block, 0)


def _diff_attention(q, k, vt, lam_arr, g_mat, tq, tk, unroll):
    B, S, _ = q.shape
    n_g = tq // LANES
    kern = functools.partial(_attn_kernel, tq=tq, tk=tk, qb=LANES, unroll=unroll)
    head_block = pl.BlockSpec((None, S, LANES), lambda b_, h: (b_, 0, h))
    return pl.pallas_call(
        kern,
        grid=(B, DA_HEADS),
        in_specs=[pl.BlockSpec(memory_space=pltpu.SMEM),
                  head_block,
                  head_block,
                  pl.BlockSpec((None, None, V_ROWS, S), lambda b_, h: (b_, h, 0, 0)),
                  pl.BlockSpec((DA_V_DIM, LANES), lambda b_, h: (0, 0))],
        out_specs=head_block,
        out_shape=jax.ShapeDtypeStruct((B, S, DA_WIDTH), F32),
        scratch_shapes=[pltpu.VMEM((n_g, LANES, 2 * LANES), BF16),
                        pltpu.VMEM((n_g, SUBLANES, 2 * LANES), F32),
                        pltpu.VMEM((n_g, V_ROWS, 2 * LANES), F32),
                        pltpu.VMEM((2, n_g, tk, 2 * LANES), F32),
                        pltpu.VMEM((2, n_g, SUBLANES, 2 * LANES), F32)],
        compiler_params=_cparams(("parallel", "parallel")),
        name="diff_attn",
    )(lam_arr, q, k, vt, g_mat)


def _memkv_kernel(m_ref, w_ref, km_ref, vm_ref):
    kv = jnp.dot(m_ref[...].astype(BF16), w_ref[...], preferred_element_type=F32)
    km_ref[...] = kv[:, 0:XA_WIDTH].astype(BF16)
    vm_ref[...] = kv[:, XA_WIDTH:2 * XA_WIDTH].astype(BF16)


def _mem_kv(mem, w_mem_kv):
    B, M, D = mem.shape
    return pl.pallas_call(
        _memkv_kernel,
        grid=(B,),
        in_specs=[pl.BlockSpec((None, M, D), lambda b_: (b_, 0, 0)),
                  _const_spec((D, 2 * XA_WIDTH))],
        out_specs=[pl.BlockSpec((None, M, XA_WIDTH), lambda b_: (b_, 0, 0)),
                   pl.BlockSpec((None, M, XA_WIDTH), lambda b_: (b_, 0, 0))],
        out_shape=[jax.ShapeDtypeStruct((B, M, XA_WIDTH), BF16),
                   jax.ShapeDtypeStruct((B, M, XA_WIDTH), BF16)],
        compiler_params=_cparams(("parallel",)),
        name="mem_kv",
    )(mem, w_mem_kv)


def _silu(z):
    return z * (1.0 / (1.0 + jnp.exp(-z)))


def _sigmoid(z):
    return 1.0 / (1.0 + jnp.exp(-z))


def _merge_kernel(x_ref, pu_ref, pup_ref, pun_ref, o_ref, km_ref, vm_ref, wg_ref, pw_ref, ps_ref,
                  wa_ref, wb_ref, wc_ref, wo_ref, g_ref, b_ref, y_ref, ue_ref, *, seq_len, alpha):
    i = pl.program_id(1)
    n_i = pl.num_programs(1)
    tm = x_ref.shape[0]
    x = x_ref[...]
    xb = x.astype(BF16)

    u = pu_ref[...]
    ue_ref[0:POOL_HALO, :] = jnp.where(i > 0, pup_ref[...], 0.0)
    ue_ref[POOL_HALO:POOL_HALO + tm, :] = u
    ue_ref[POOL_HALO + tm:2 * POOL_HALO + tm, :] = jnp.where(i < n_i - 1, pun_ref[...], 0.0)
    row = lax.broadcasted_iota(jnp.int32, (POOL_HALO, POOL_GROUP_WIDTH), 0) + i * tm
    t_head = row.astype(F32)
    t_tail = (row + (tm - POOL_HALO)).astype(F32)

    def window_rows(t, w):
        return w - jnp.maximum(w // 2 - t, 0.0) - jnp.maximum(t - (seq_len - w // 2), 0.0)

    pool_d = []
    for g, w in enumerate(POOL_WINDOWS):
        assert w & (w - 1) == 0 and w // 2 <= POOL_HALO
        cols = slice(g * POOL_GROUP_WIDTH, (g + 1) * POOL_GROUP_WIDTH)
        part, span = ue_ref[:, cols], 1
        while span < w:
            n = part.shape[0] - span
            part, span = part[0:n] + part[span:span + n], 2 * span
        r0 = POOL_HALO - w // 2
        wsum = part[r0:r0 + tm]
        mean = jnp.concatenate([wsum[0:POOL_HALO] / window_rows(t_head, w),
                                wsum[POOL_HALO:tm - POOL_HALO] * (1.0 / w),
                                wsum[tm - POOL_HALO:tm] / window_rows(t_tail, w)], axis=0)
        pool_d.append((mean - u[:, cols]).astype(BF16))

    halves = (slice(0, tm // 2), slice(tm // 2, tm))

    def gate(r, c0, c1):
        return jnp.dot(xb[r], wg_ref[:, c0:c1], preferred_element_type=F32)

    def dot(a, w_ref):
        return jnp.dot(a, w_ref[...], preferred_element_type=F32)

    xq = [gate(r, G_XQ, G_XZ).astype(BF16) for r in halves]
    head_cols = [slice(h * XA_HEAD_DIM, (h + 1) * XA_HEAD_DIM) for h in range(XA_HEADS)]
    sc = [[lax.dot_general(q[:, c], km_ref[:, c], (((1,), (1,)), ((), ())), preferred_element_type=F32)
           * (XA_HEAD_DIM ** -0.5) for c in head_cols] for q in xq]

    yb = [(o_ref[r, :] * _silu(gate(r, G_AZ, G_XQ))).astype(BF16) for r in halves]
    yb = [dot(y, wb_ref) for y in yb]
    z_pool = [gate(r, G_PZ, G_AZ) for r in halves]
    z_mem = [gate(r, G_XZ, G_GA) for r in halves]

    yc = []
    for k in range(2):
        parts = []
        for h, c in enumerate(head_cols):
            e = jnp.exp(sc[k][h] - jnp.max(sc[k][h], axis=-1, keepdims=True))
            pm = (e / jnp.sum(e, axis=-1, keepdims=True)).astype(BF16)
            parts.append(jnp.dot(pm, vm_ref[:, c], preferred_element_type=F32))
        yc.append((jnp.concatenate(parts, axis=1) * _silu(z_mem[k])).astype(BF16))
    g_a = [_sigmoid(gate(r, G_GA, G_GB)) for r in halves]
    yc = [dot(y, wc_ref) for y in yc]

    ya = [(jnp.concatenate([jnp.dot(pool_d[g][r], pw_ref[g], preferred_element_type=F32)
                            for g in range(POOL_GROUPS)], axis=1) * ps_ref[...] * _silu(z_pool[k])).astype(BF16)
          for k, r in enumerate(halves)]
    g_b = [_sigmoid(gate(r, G_GB, G_GC)) for r in halves]
    ya = [dot(y, wa_ref) for y in ya]

    for k, r in enumerate(halves):
        merged = (g_a[k] * ya[k] + g_b[k] * yb[k] + _sigmoid(gate(r, G_GC, G_END)) * yc[k]).astype(BF16)
        y_ref[r, :] = _ln_rows(alpha * x[r] + dot(merged, wo_ref), g_ref[...], b_ref[...])


def _merge(x, pu, o, km, vm, w_g, pool_w, pool_scale, w_a, w_b, w_c, w_o, ln_g, ln_b, tm, alpha):
    B, S, D = x.shape
    hb = tm // POOL_HALO
    n_hb = S // POOL_HALO
    kern = functools.partial(_merge_kernel, seq_len=S, alpha=alpha)
    tile = lambda w: pl.BlockSpec((None, tm, w), lambda b_, i: (b_, i, 0))
    return pl.pallas_call(
        kern,
        grid=(B, S // tm),
        in_specs=[tile(D),
                  tile(POOL_WIDTH),
                  pl.BlockSpec((None, POOL_HALO, POOL_WIDTH), lambda b_, i: (b_, jnp.maximum(i * hb - 1, 0), 0)),
                  pl.BlockSpec((None, POOL_HALO, POOL_WIDTH),
                               lambda b_, i: (b_, jnp.minimum((i + 1) * hb, n_hb - 1), 0)),
                  tile(DA_WIDTH),
                  pl.BlockSpec((None, N_MEM, XA_WIDTH), lambda b_, i: (b_, 0, 0)),
                  pl.BlockSpec((None, N_MEM, XA_WIDTH), lambda b_, i: (b_, 0, 0)),
                  _const_spec(w_g.shape),
                  _const_spec(pool_w.shape),
                  _const_spec((1, POOL_WIDTH)),
                  _const_spec(w_a.shape), _const_spec(w_b.shape), _const_spec(w_c.shape), _const_spec(w_o.shape),
                  _const_spec((1, D)), _const_spec((1, D))],
        out_specs=tile(D),
        out_shape=jax.ShapeDtypeStruct((B, S, D), F32),
        scratch_shapes=[pltpu.VMEM((tm + 2 * POOL_HALO, POOL_WIDTH), F32)],
        compiler_params=_cparams(("parallel", "arbitrary")),
        name="merge",
    )(x, pu, pu, pu, o, km, vm, w_g, pool_w, pool_scale.reshape(1, POOL_WIDTH), w_a, w_b, w_c, w_o,
      ln_g.reshape(1, D), ln_b.reshape(1, D))


def _rope_tables(seq):
    pos = jnp.arange(seq, dtype=F32)
    inv = ROPE_THETA ** (-jnp.arange(0, ROT_DIM, 2, dtype=F32) / ROT_DIM)
    ang = pos[:, None] * inv[None, :]
    cos, sin = jnp.cos(ang), jnp.sin(ang)
    half = ROT_DIM // 2
    lane = jnp.arange(LANES) % DA_HEAD_DIM
    idx = lane % half
    c = jnp.where(lane[None, :] < ROT_DIM, cos[:, idx], 1.0)
    sa = jnp.where((lane[None, :] >= half) & (lane[None, :] < ROT_DIM), sin[:, idx], 0.0)
    sb = jnp.where(lane[None, :] < half, -sin[:, idx], 0.0)
    return c.astype(F32), sa.astype(F32), sb.astype(F32)


def _tiles(S):
    if S >= 8192:
        attn = dict(tq=1024, tk=512, unroll=4)
    else:
        attn = dict(tq=2048, tk=512, unroll=2)
    return dict(tm_ln=512, tm_proj=512, tm_merge=512, **attn)


def _trunk(x, mem, ln_in_g, ln_in_b, layers):
    B, S, _ = x.shape
    depth = len(layers)
    alpha = (2.0 * depth) ** 0.25
    t = _tiles(S)
    rope_c, rope_sa, rope_sb = _rope_tables(S)
    x = _layernorm(x, ln_in_g, ln_in_b, t["tm_ln"])
    for L in layers:
        pu, q, k, vt = _project(x, L["w_a"], rope_c, rope_sa, rope_sb, t["tm_proj"])
        o = _diff_attention(q, k, vt, L["lam"], L["subln"], t["tq"], t["tk"], t["unroll"])
        km, vm = _mem_kv(mem, L["w_mem_kv"])
        x = _merge(x, pu, o, km, vm, L["w_g"], L["pool_w"], L["pool_scale"], L["w_br_a"], L["w_br_b"],
                   L["w_br_c"], L["w_out"], L["ln_g"], L["ln_b"], t["tm_merge"], alpha)
    return x


def _prepare_layers(w_in, w_mem_kv, pool_w, pool_scale, lam_q1, lam_k1, lam_q2, lam_k2, subln_g,
                    w_br_a, w_br_b, w_br_c, w_out, ln_g, ln_b):
    layers = []
    for i in range(w_in.shape[0]):
        lam_init = 0.8 - 0.6 * math.exp(-0.3 * i)
        lam = (jnp.exp(jnp.sum(lam_q1[i] * lam_k1[i])) - jnp.exp(jnp.sum(lam_q2[i] * lam_k2[i])) + lam_init)
        wi = w_in[i].astype(BF16)
        layers.append(dict(
            w_a=jnp.concatenate([wi[:, C_PU:C_PZ], wi[:, C_Q:C_AZ]], axis=1),
            w_g=jnp.concatenate([wi[:, C_PZ:C_Q], wi[:, C_AZ:C_END]], axis=1),
            w_mem_kv=w_mem_kv[i].astype(BF16),
            pool_w=pool_w[i].astype(BF16),
            pool_scale=pool_scale[i],
            lam=jnp.reshape(lam, (1,)).astype(F32),
            subln=jnp.broadcast_to((subln_g[i] * (1.0 - lam_init))[:, None], (DA_V_DIM, LANES)).astype(F32),
            w_br_a=w_br_a[i].astype(BF16), w_br_b=w_br_b[i].astype(BF16), w_br_c=w_br_c[i].astype(BF16),
            w_out=w_out[i].astype(BF16), ln_g=ln_g[i], ln_b=ln_b[i]))
    return layers


def kernel(x_prompt, x_sample, mem_prompt, mem_sample, ln_in_g, ln_in_b, w_in, w_mem_kv, pool_w, pool_scale,
           lam_q1, lam_k1, lam_q2, lam_k2, subln_g, w_br_a, w_br_b, w_br_c, w_out, ln_g, ln_b):
    layers = _prepare_layers(w_in, w_mem_kv, pool_w, pool_scale, lam_q1, lam_k1, lam_q2, lam_k2, subln_g,
                             w_br_a, w_br_b, w_br_c, w_out, ln_g, ln_b)
    y_prompt = _trunk(x_prompt, mem_prompt, ln_in_g, ln_in_b, layers)
    y_sample = _trunk(x_sample, mem_sample, ln_in_g, ln_in_b, layers)
    return (y_prompt, y_sample)
```
